```python
import math
import jax, jax.numpy as jnp
from jax import lax
import numpy as np

D_MODEL = 1024
BATCH = 8
SEQ = 4096
DEPTH = 2

N_META = 16
N_MIXERS = 2
EXPAND = 2
D_INNER = EXPAND * D_MODEL
RMS_EPS = 1e-6
N_RWKV_LAYERS = (DEPTH + 1) // 2
N_MLA_LAYERS = DEPTH // 2

RWKV_HEAD = 64
RWKV_HEADS = D_INNER // RWKV_HEAD
DECAY_LORA = 64
AAA_LORA = 64
RWKV_WIDTHS = (D_INNER, D_INNER, D_INNER, D_INNER, DECAY_LORA, AAA_LORA)
N_LERP = len(RWKV_WIDTHS)
RWKV_IN = sum(RWKV_WIDTHS)
GN_EPS = 64e-5

MLA_HEADS = 16
QK_NOPE = 128
QK_ROPE = 64
V_HEAD = 128
Q_LORA = 512
KV_LORA = 256
MLA_IN = Q_LORA + KV_LORA + QK_ROPE + D_INNER
ROPE_THETA = 10000.0
Q_BLOCK = 128

kernel_name = "hybrid_rwkv7_mla_meta_sandwich"


def _rms_norm(x, g):
    xf = x.astype(jnp.float32)
    y = xf * lax.rsqrt(jnp.mean(xf * xf, axis=-1, keepdims=True) + RMS_EPS)
    return (y * g.astype(jnp.float32)).astype(x.dtype)


def _rope(x, pos):
    half = x.shape[-1] // 2
    inv_freq = jnp.exp(-math.log(ROPE_THETA) * jnp.arange(half, dtype=jnp.float32) / half)
    ang = pos[:, None] * inv_freq[None, :]
    ang = ang.reshape((1, ang.shape[0]) + (1,) * (x.ndim - 3) + (half,))
    cos, sin = jnp.cos(ang), jnp.sin(ang)
    xf = x.astype(jnp.float32)
    x1, x2 = xf[..., :half], xf[..., half:]
    return jnp.concatenate([x1 * cos - x2 * sin, x2 * cos + x1 * sin], axis=-1).astype(x.dtype)


def _wkv7_scan(r, decay, k, v, a_vec, b_vec):
    B, L, H, N = r.shape
    xs = tuple(jnp.moveaxis(t.astype(jnp.float32), 1, 0) for t in (r, decay, k, v, a_vec, b_vec))

    def step(S, inp):
        r_t, w_t, k_t, v_t, a_t, b_t = inp
        sa = jnp.einsum('bhij,bhj->bhi', S, a_t)
        S = S * w_t[:, :, None, :] + sa[..., None] * b_t[:, :, None, :] + v_t[..., None] * k_t[:, :, None, :]
        return S, jnp.einsum('bhij,bhj->bhi', S, r_t)

    S0 = jnp.zeros((B, H, N, N), jnp.float32)
    _, ys = lax.scan(step, S0, xs)
    return jnp.moveaxis(ys, 0, 1)


def _rwkv7_mixer(x, mu, w_in, w0, w2, a0, a2, k_k, k_a, r_k, ln_w, ln_b, w_out):
    B, L, _ = x.shape
    H, N = RWKV_HEADS, RWKV_HEAD
    x_prev = jnp.pad(x, ((0, 0), (1, 0), (0, 0)))[:, :-1]
    dx = x_prev - x
    widths = np.array(RWKV_WIDTHS)
    mu_cols = jnp.repeat(mu.T, widths, axis=1, total_repeat_length=RWKV_IN)
    w_comb = jnp.concatenate([w_in, mu_cols * w_in], axis=0)
    h = jnp.concatenate([x, dx], axis=-1) @ w_comb
    r, k, v, g, w_lo, a_lo = jnp.split(h, list(np.cumsum(widths)[:-1]), axis=-1)
    w = -jax.nn.softplus(-(w0 + jnp.tanh(w_lo) @ w2)) - 0.5
    a = jax.nn.sigmoid(a0 + a_lo @ a2)
    heads = lambda t: t.reshape(B, L, H, N)
    r, k, v, w, a = heads(r), heads(k), heads(v), heads(w), heads(a)
    k_k, k_a, r_k = k_k.reshape(H, N), k_a.reshape(H, N), r_k.reshape(H, N)
    kk = (k * k_k).astype(jnp.float32)
    kk = kk / jnp.maximum(jnp.linalg.norm(kk, axis=-1, keepdims=True), 1e-12)
    k = k * (1 + (a - 1) * k_a)
    decay = jnp.exp(-jnp.exp(w.astype(jnp.float32)))
    y = _wkv7_scan(r, decay, k, v, -kk, kk * a.astype(jnp.float32))
    mean = jnp.mean(y, axis=-1, keepdims=True)
    var = jnp.mean(jnp.square(y - mean), axis=-1, keepdims=True)
    y = ((y - mean) * lax.rsqrt(var + GN_EPS)).reshape(B, L, D_INNER)
    y = y * ln_w.astype(jnp.float32) + ln_b.astype(jnp.float32)
    bonus = (jnp.sum(r * k * r_k, axis=-1, keepdims=True) * v).reshape(B, L, D_INNER)
    y = (y + bonus.astype(jnp.float32)).astype(x.dtype) * jax.nn.silu(g)
    return y @ w_out


def _causal_block_attention(q_nope, q_rope, k_nope, k_rope, v):
    L = q_nope.shape[1]
    scale = (QK_NOPE + QK_ROPE) ** -0.5
    neg = jnp.finfo(jnp.float32).min
    bounds = [(0, N_META)] + [(s, min(s + Q_BLOCK, L)) for s in range(N_META, L, Q_BLOCK)]
    outs = []
    for s, e in bounds:
        sc = (jnp.einsum('bqhd,bkhd->bhqk', q_nope[:, s:e], k_nope[:, :e])
              + jnp.einsum('bqhr,bkr->bhqk', q_rope[:, s:e], k_rope[:, :e])).astype(jnp.float32) * scale
        mask = jnp.arange(s, e)[:, None] >= jnp.arange(e)[None, :]
        p = jax.nn.softmax(jnp.where(mask, sc, neg), axis=-1).astype(v.dtype)
        outs.append(jnp.einsum('bhqk,bkhd->bqhd', p, v[:, :e]))
    return jnp.concatenate(outs, axis=1)


def _mla_mixer(x, pos, w_in, q_norm, w_q_up, kv_norm, w_kv_up, w_out):
    B, L, _ = x.shape
    H = MLA_HEADS
    h = x @ w_in
    c_q, c_kv, k_rope, g = jnp.split(h, [Q_LORA, Q_LORA + KV_LORA, Q_LORA + KV_LORA + QK_ROPE], axis=-1)
    q = (_rms_norm(c_q, q_norm) @ w_q_up).reshape(B, L, H, QK_NOPE + QK_ROPE)
    q_nope, q_rope = q[..., :QK_NOPE], _rope(q[..., QK_NOPE:], pos)
    kv = (_rms_norm(c_kv, kv_norm) @ w_kv_up).reshape(B, L, H, QK_NOPE + V_HEAD)
    k_nope, v = kv[..., :QK_NOPE], kv[..., QK_NOPE:]
    k_rope = _rope(k_rope, pos)
    o = _causal_block_attention(q_nope, q_rope, k_nope, k_rope, v).reshape(B, L, D_INNER)
    return (o * jax.nn.silu(g)) @ w_out


def setup_inputs(seed: int = 0) -> dict:
    key = jax.random.key(seed)
    ks = jax.random.split(key, 24)
    f32 = jnp.float32
    nrm = lambda k, shape, s: jax.random.normal(k, shape, f32) * s
    NR, NM = N_RWKV_LAYERS, N_MLA_LAYERS
    return {
        "x": nrm(ks[0], (BATCH, SEQ, D_MODEL), 1.0),
        "meta_tokens": nrm(ks[1], (N_META, D_MODEL), 1.0),
        "norm_pre": 1.0 + nrm(ks[2], (DEPTH, D_MODEL), 0.02),
        "norm_post": 1.0 + nrm(ks[3], (DEPTH, D_MODEL), 0.02),
        "rwkv_mu": jax.random.uniform(ks[4], (NR, N_LERP, D_MODEL), f32),
        "rwkv_w_in": nrm(ks[5], (NR, D_MODEL, RWKV_IN), D_MODEL ** -0.5),
        "rwkv_w0": -2.0 + nrm(ks[6], (NR, D_INNER), 0.5),
        "rwkv_w2": nrm(ks[7], (NR, DECAY_LORA, D_INNER), 0.5 * DECAY_LORA ** -0.5),
        "rwkv_a0": nrm(ks[8], (NR, D_INNER), 0.1),
        "rwkv_a2": nrm(ks[9], (NR, AAA_LORA, D_INNER), 0.5 * AAA_LORA ** -0.5),
        "rwkv_k_k": 0.85 + nrm(ks[10], (NR, D_INNER), 0.05),
        "rwkv_k_a": 1.0 + nrm(ks[11], (NR, D_INNER), 0.05),
        "rwkv_r_k": nrm(ks[12], (NR, D_INNER), 0.1),
        "rwkv_ln_w": 1.0 + nrm(ks[13], (NR, D_INNER), 0.02),
        "rwkv_ln_b": nrm(ks[14], (NR, D_INNER), 0.02),
        "rwkv_w_out": nrm(ks[15], (NR, D_INNER, D_MODEL), D_INNER ** -0.5),
        "mla_w_in": nrm(ks[16], (NM, D_MODEL, MLA_IN), D_MODEL ** -0.5),
        "mla_q_norm": 1.0 + nrm(ks[17], (NM, Q_LORA), 0.02),
        "mla_w_q_up": nrm(ks[18], (NM, Q_LORA, MLA_HEADS * (QK_NOPE + QK_ROPE)), Q_LORA ** -0.5),
        "mla_kv_norm": 1.0 + nrm(ks[19], (NM, KV_LORA), 0.02),
        "mla_w_kv_up": nrm(ks[20], (NM, KV_LORA, MLA_HEADS * (QK_NOPE + V_HEAD)), KV_LORA ** -0.5),
        "mla_w_out": nrm(ks[21], (NM, D_INNER, D_MODEL), D_INNER ** -0.5),
    }


def reference(x, meta_tokens, norm_pre, norm_post,
              rwkv_mu, rwkv_w_in, rwkv_w0, rwkv_w2, rwkv_a0, rwkv_a2,
              rwkv_k_k, rwkv_k_a, rwkv_r_k, rwkv_ln_w, rwkv_ln_b, rwkv_w_out,
              mla_w_in, mla_q_norm, mla_w_q_up, mla_kv_norm, mla_w_kv_up, mla_w_out):
    B = x.shape[0]
    meta = jnp.broadcast_to(meta_tokens.astype(x.dtype)[None], (B, N_META, D_MODEL))
    h = jnp.concatenate([meta, x], axis=1)
    pos = jnp.arange(h.shape[1], dtype=jnp.float32)
    for i in range(DEPTH):
        j = i // N_MIXERS
        u = _rms_norm(h, norm_pre[i])
        if i % N_MIXERS == 0:
            m = _rwkv7_mixer(u, rwkv_mu[j], rwkv_w_in[j], rwkv_w0[j], rwkv_w2[j], rwkv_a0[j], rwkv_a2[j],
                             rwkv_k_k[j], rwkv_k_a[j], rwkv_r_k[j], rwkv_ln_w[j], rwkv_ln_b[j], rwkv_w_out[j])
        else:
            m = _mla_mixer(u, pos, mla_w_in[j], mla_q_norm[j], mla_w_q_up[j], mla_kv_norm[j],
                           mla_w_kv_up[j], mla_w_out[j])
        h = h + _rms_norm(m, norm_post[i])
    return h[:, N_META:]
```

```python
import functools
import math

import jax
import jax.numpy as jnp
from jax import lax
from jax.experimental import pallas as pl
from jax.experimental.pallas import tpu as pltpu

F32 = jnp.float32
BF16 = jnp.bfloat16

RMS_EPS = 1e-6
GN_EPS = 64e-5
ROPE_THETA = 10000.0
N_META = 16
HEAD = 64
LANES = 128
CHUNK = 64
QK_NOPE = 128
QK_ROPE = 64
V_HEAD = 128
VMEM_LIMIT = 56 * 1024 * 1024


def _cparams(sem):
    return pltpu.CompilerParams(dimension_semantics=sem, vmem_limit_bytes=VMEM_LIMIT)


def _rms(x, g):
    return x * lax.rsqrt(jnp.mean(x * x, axis=-1, keepdims=True) + RMS_EPS) * g


def _dot(a, b):
    return jnp.dot(a.astype(BF16), b.astype(BF16), preferred_element_type=F32)


def _dot_nt(a, b):
    return lax.dot_general(a.astype(BF16), b.astype(BF16), (((1,), (1,)), ((), ())),
                           preferred_element_type=F32)


def _dot_tn(a, b):
    return lax.dot_general(a.astype(BF16), b.astype(BF16), (((0,), (0,)), ((), ())),
                           preferred_element_type=F32)


def _inproj_kernel(x_ref, xprev_ref, x0_ref, g_ref, mu_ref, mul_ref, w_ref, wl_ref,
                   out_ref, outl_ref, u_sc, dx_sc, *, tiles_per_seq):
    i = pl.program_id(0)
    j = pl.program_id(1)

    @pl.when(j == 0)
    def _():
        g = g_ref[...]
        u = _rms(x_ref[...], g)
        first = (i % tiles_per_seq) == 0
        prow = jnp.where(first, x0_ref[7:8, :], xprev_ref[7:8, :])
        up = _rms(prow, g)
        rows = lax.broadcasted_iota(jnp.int32, u.shape, 0)
        ush = jnp.where(rows == 0, up, pltpu.roll(u, 1, 0))
        dx = ush - u
        u_sc[...] = u
        dx_sc[...] = dx
        wl = wl_ref[...]
        ow = _dot(u + mul_ref[0] * dx, wl)
        oa = _dot(u + mul_ref[1] * dx, wl)
        lane = lax.broadcasted_iota(jnp.int32, ow.shape, 1)
        outl_ref[...] = jnp.where(lane < HEAD, ow, oa).astype(outl_ref.dtype)

    xg = u_sc[...] + mu_ref[0] * dx_sc[...]
    out_ref[...] = _dot(xg, w_ref[...]).astype(out_ref.dtype)


def _inproj(x2, x0, g, mu_main, mu_lora, w_main, w_lora, *, seq, tr):
    rows, d = x2.shape
    ncol = w_main.shape[1]
    ngroups = mu_main.shape[0]
    gw = ncol // ngroups
    tpb = tr // 8
    kern = functools.partial(_inproj_kernel, tiles_per_seq=seq // tr)
    return pl.pallas_call(
        kern,
        grid=(rows // tr, ngroups),
        in_specs=[
            pl.BlockSpec((tr, d), lambda i, j: (i, 0)),
            pl.BlockSpec((8, d), lambda i, j: (jnp.maximum(i * tpb - 1, 0), 0)),
            pl.BlockSpec((8, d), lambda i, j: (0, 0)),
            pl.BlockSpec((1, d), lambda i, j: (0, 0)),
            pl.BlockSpec((1, 1, d), lambda i, j: (j, 0, 0)),
            pl.BlockSpec((2, 1, d), lambda i, j: (0, 0, 0)),
            pl.BlockSpec((d, gw), lambda i, j: (0, j)),
            pl.BlockSpec((d, LANES), lambda i, j: (0, 0)),
        ],
        out_specs=[
            pl.BlockSpec((tr, gw), lambda i, j: (i, j)),
            pl.BlockSpec((tr, LANES), lambda i, j: (i, 0)),
        ],
        out_shape=[
            jax.ShapeDtypeStruct((rows, ncol), BF16),
            jax.ShapeDtypeStruct((rows, LANES), BF16),
        ],
        scratch_shapes=[pltpu.VMEM((tr, d), F32), pltpu.VMEM((tr, d), F32)],
        compiler_params=_cparams(("arbitrary", "arbitrary")),
        name="rwkv_inproj",
    )(x2, x2, x0, g, mu_main, mu_lora, w_main, w_lora)


def _bd(x, m0):
    zero = jnp.zeros_like(x)
    return jnp.concatenate([jnp.where(m0, x, zero), jnp.where(m0, zero, x)], axis=0)


def _wkv_kernel(r_ref, k_ref, v_ref, lo_ref, par_ref, w2a_ref, s0_ref, y_ref, sout_ref,
                s_sc, y_sc, *, nchunks):
    tb = pl.program_id(2)
    C = CHUNK

    @pl.when(tb == 0)
    def _():
        s_sc[...] = s0_ref[0]

    par = par_ref[0]
    w0, a0, k_k, k_a, r_k, ln_w, ln_b = [par[n:n + 1] for n in range(7)]
    r = r_ref[0].astype(F32)
    k = k_ref[0].astype(F32)
    v = v_ref[0].astype(F32)
    lo = lo_ref[0].astype(F32)
    tbr = r.shape[0]

    lane_t = lax.broadcasted_iota(jnp.int32, (tbr, LANES), 1)
    tl = jnp.where(lane_t < HEAD, jnp.tanh(lo), lo)
    wa = _dot(tl, w2a_ref[0])
    w = -jax.nn.softplus(-(w0 + wa[:, :LANES])) - 0.5
    lw = -jnp.exp(w)
    a = jax.nn.sigmoid(a0 + wa[:, LANES:])

    row2 = lax.broadcasted_iota(jnp.int32, (LANES, LANES), 0)
    lane2 = lax.broadcasted_iota(jnp.int32, (LANES, LANES), 1)
    samehead = (row2 // HEAD) == (lane2 // HEAD)
    ones_bd = jnp.where(samehead, 1.0, 0.0).astype(BF16)
    avg_bd = jnp.where(samehead, 1.0 / HEAD, 0.0).astype(BF16)

    kk = k * k_k
    kkn = kk / jnp.maximum(jnp.sqrt(_dot(kk * kk, ones_bd)), 1e-12)
    km = k * (1.0 + (a - 1.0) * k_a)
    bonus = _dot(r * km * r_k, ones_bd) * v
    av = -kkn
    bv = kkn * a

    tpos = lax.broadcasted_iota(jnp.int32, (C, LANES), 0)
    lane_c = lax.broadcasted_iota(jnp.int32, (C, LANES), 1)
    spos = lane_c % HEAD
    m0 = lane_c < HEAD
    m0_2 = lax.broadcasted_iota(jnp.int32, (2 * C, LANES), 1) < HEAD
    strict = tpos > spos
    incl = tpos >= spos
    eye = jnp.where(tpos == spos, 1.0, 0.0).astype(F32)
    tri = jnp.where(lax.broadcasted_iota(jnp.int32, (C, C), 0)
                    >= lax.broadcasted_iota(jnp.int32, (C, C), 1), 1.0, 0.0).astype(BF16)

    s = s_sc[...]
    for c in range(nchunks):
        sl = slice(c * C, (c + 1) * C)
        lwc = lw[sl]
        hi = lwc.astype(BF16)
        lo2 = (lwc - hi.astype(F32)).astype(BF16)
        cs2 = jnp.dot(tri, jnp.concatenate([hi, lo2], axis=1), preferred_element_type=F32)
        cs = cs2[:, :LANES] + cs2[:, LANES:]
        last = cs[C - 1:C]
        e_in = jnp.exp(cs)
        e_out = jnp.exp(-cs)
        e_ex = jnp.exp(cs - lwc)
        e_l = jnp.exp(last - cs)
        wc = jnp.exp(last)
        rt = r[sl] * e_in
        at = av[sl] * e_ex
        kt = km[sl] * e_out
        bt = bv[sl] * e_out
        kp = km[sl] * e_l
        bp = bv[sl] * e_l
        vc = v[sl]

        sc = _dot_nt(jnp.concatenate([at, rt], axis=0),
                     jnp.concatenate([_bd(bt, m0), _bd(kt, m0)], axis=0))
        ab = jnp.where(strict, sc[:C, :LANES], 0.0)
        ak = jnp.where(strict, sc[:C, LANES:], 0.0)
        rb = jnp.where(incl, sc[C:, :LANES], 0.0)
        rk = jnp.where(incl, sc[C:, LANES:], 0.0)

        p = eye + ab
        q = _dot(ab, _bd(ab, m0))
        for _ in range(4):
            pq = _dot(jnp.concatenate([p, q], axis=0), _bd(q, m0))
            p = p + pq[:C]
            q = pq[C:]
        p = p + _dot(p, _bd(q, m0))

        lv = _dot(jnp.concatenate([ak, rk], axis=0), _bd(vc, m0))
        akv = lv[:C]
        rkv = lv[C:]
        tt = _dot(p, jnp.concatenate([_bd(at, m0), _bd(akv, m0)], axis=1))
        ta = tt[:, :LANES]
        tlv = tt[:, LANES:]
        qy = _dot(rb, jnp.concatenate([_bd(ta, m0), _bd(tlv, m0)], axis=1))
        qm = rt + qy[:, :LANES]
        yl = qy[:, LANES:] + rkv

        mfull = _dot_tn(ta, bp)
        mbd = jnp.where(samehead, mfull, 0.0)
        hfull = _dot_tn(jnp.concatenate([tlv, vc], axis=0), jnp.concatenate([bp, kp], axis=0))
        hh = jnp.where(m0, hfull[:HEAD], hfull[HEAD:])

        y_sc[sl, :] = yl + _dot_nt(qm, _bd(s, m0))
        s = s * wc + _dot(s, mbd) + hh

    s_sc[...] = s
    sout_ref[0, 0] = s

    y = y_sc[...]
    mean = _dot(y, avg_bd)
    d = y - mean
    var = _dot(d * d, avg_bd)
    yn = d * lax.rsqrt(var + GN_EPS) * ln_w + ln_b + bonus
    y_ref[0] = yn.astype(y_ref.dtype)


def _wkv(hp, hl, par, w2a, s0, *, batch, seq, tb):
    d_inner = hp.shape[1] // 4
    npairs = d_inner // LANES
    hp3 = hp.reshape(batch, seq, 4 * d_inner)
    hl3 = hl.reshape(batch, seq, LANES)
    kern = functools.partial(_wkv_kernel, nchunks=tb // CHUNK)
    return pl.pallas_call(
        kern,
        grid=(batch, npairs, seq // tb),
        in_specs=[
            pl.BlockSpec((1, tb, LANES), lambda b, p, t: (b, t, p)),
            pl.BlockSpec((1, tb, LANES), lambda b, p, t: (b, t, npairs + p)),
            pl.BlockSpec((1, tb, LANES), lambda b, p, t: (b, t, 2 * npairs + p)),
            pl.BlockSpec((1, tb, LANES), lambda b, p, t: (b, t, 0)),
            pl.BlockSpec((1, 8, LANES), lambda b, p, t: (p, 0, 0)),
            pl.BlockSpec((1, LANES, 2 * LANES), lambda b, p, t: (p, 0, 0)),
            pl.BlockSpec((1, HEAD, LANES), lambda b, p, t: (p, 0, 0)),
        ],
        out_specs=[
            pl.BlockSpec((1, tb, LANES), lambda b, p, t: (b, t, p)),
            pl.BlockSpec((1, 1, HEAD, LANES), lambda b, p, t: (b, p, 0, 0)),
        ],
        out_shape=[
            jax.ShapeDtypeStruct((batch, seq, d_inner), BF16),
            jax.ShapeDtypeStruct((batch, npairs, HEAD, LANES), F32),
        ],
        scratch_shapes=[pltpu.VMEM((HEAD, LANES), F32), pltpu.VMEM((tb, LANES), F32)],
        compiler_params=_cparams(("arbitrary", "arbitrary", "arbitrary")),
        name="wkv7",
    )(hp3, hp3, hp3, hl3, par, w2a, s0)


def _outproj_kernel(y_ref, g_ref, res_ref, w_ref, gn_ref, out_ref):
    g = g_ref[...].astype(F32)
    z = y_ref[...].astype(F32) * (g * jax.nn.sigmoid(g))
    m = _dot(z, w_ref[...])
    out_ref[...] = res_ref[...] + _rms(m, gn_ref[...])


def _outproj(y2, gsrc, gcol, res2, w, gn, *, tr):
    rows, di = y2.shape
    d = w.shape[1]
    return pl.pallas_call(
        _outproj_kernel,
        grid=(rows // tr,),
        in_specs=[
            pl.BlockSpec((tr, di), lambda i: (i, 0)),
            pl.BlockSpec((tr, di), lambda i: (i, gcol)),
            pl.BlockSpec((tr, d), lambda i: (i, 0)),
            pl.BlockSpec((di, d), lambda i: (0, 0)),
            pl.BlockSpec((1, d), lambda i: (0, 0)),
        ],
        out_specs=pl.BlockSpec((tr, d), lambda i: (i, 0)),
        out_shape=jax.ShapeDtypeStruct((rows, d), F32),
        compiler_params=_cparams(("arbitrary",)),
        name="gate_outproj",
    )(y2, gsrc, res2, w, gn)


def _mlaproj_kernel(h_ref, gpre_ref, win_ref, qn_ref, wq_ref, kvn_ref, wkv_ref, rc_ref, rs_ref,
                    oqn_ref, oqr_ref, okn_ref, ov_ref, okr_ref, og_ref, *, scale):
    u = _rms(h_ref[...], gpre_ref[...])
    t = _dot(u, win_ref[...])
    nq = qn_ref.shape[1]
    nkv = kvn_ref.shape[1]
    di = og_ref.shape[1]
    c_q = t[:, :nq]
    c_kv = t[:, nq:nq + nkv]
    krr = t[:, nq + nkv:nq + nkv + LANES]
    og_ref[...] = t[:, nq + nkv + LANES:].astype(og_ref.dtype)

    rc = rc_ref[...]
    rs = rs_ref[...]
    lane = lax.broadcasted_iota(jnp.int32, krr.shape, 1)
    prod = krr * jnp.where(lane < QK_ROPE, rc, rs)
    okr_ref[...] = (prod + pltpu.roll(prod, QK_ROPE, 1)).astype(okr_ref.dtype)

    q = _dot(_rms(c_q, qn_ref[...]), wq_ref[...]) * scale
    oqn_ref[...] = q[:, :di].astype(oqn_ref.dtype)
    nrep = (di // 2) // LANES
    rct = jnp.concatenate([rc] * nrep, axis=1)
    rst = jnp.concatenate([rs] * nrep, axis=1)
    oqr_ref[...] = (q[:, di:di + di // 2] * rct + q[:, di + di // 2:] * rst).astype(oqr_ref.dtype)

    kv = _dot(_rms(c_kv, kvn_ref[...]), wkv_ref[...])
    okn_ref[...] = kv[:, :di].astype(okn_ref.dtype)
    ov_ref[...] = kv[:, di:].astype(ov_ref.dtype)


def _mlaproj(h2, gpre, win, qn, wq, kvn, wkv, rc, rs, *, tr, scale):
    rows, d = h2.shape
    di = wkv.shape[1] // 2
    full = lambda a: pl.BlockSpec(a.shape, lambda i: (0, 0))
    rowblk = lambda n: pl.BlockSpec((tr, n), lambda i: (i, 0))
    kern = functools.partial(_mlaproj_kernel, scale=scale)
    return pl.pallas_call(
        kern,
        grid=(rows // tr,),
        in_specs=[rowblk(d), full(gpre), full(win), full(qn), full(wq), full(kvn), full(wkv),
                  rowblk(LANES), rowblk(LANES)],
        out_specs=[rowblk(di), rowblk(di // 2), rowblk(di), rowblk(di), rowblk(LANES), rowblk(di)],
        out_shape=[
            jax.ShapeDtypeStruct((rows, di), BF16),
            jax.ShapeDtypeStruct((rows, di // 2), BF16),
            jax.ShapeDtypeStruct((rows, di), BF16),
            jax.ShapeDtypeStruct((rows, di), BF16),
            jax.ShapeDtypeStruct((rows, LANES), BF16),
            jax.ShapeDtypeStruct((rows, di), BF16),
        ],
        compiler_params=_cparams(("arbitrary",)),
        name="mla_proj",
    )(h2, gpre, win, qn, wq, kvn, wkv, rc, rs)


NEG = -1e30


def _attn_kernel(qn_ref, qr_ref, kn_ref, kr_ref, v_ref, knm_ref, krm_ref, vm_ref, o_ref, *, tq, nmeta):
    qi = pl.program_id(2)
    lane = lax.broadcasted_iota(jnp.int32, (tq, LANES), 1)
    qr = qr_ref[0]
    rowp = lax.broadcasted_iota(jnp.int32, (tq, tq), 0)
    colp = lax.broadcasted_iota(jnp.int32, (tq, tq), 1)
    causal = colp <= rowp
    metacol = lax.broadcasted_iota(jnp.int32, (tq, knm_ref.shape[1]), 1) < nmeta

    for h in range(2):
        hs = slice(h * LANES, (h + 1) * LANES)
        q = jnp.concatenate(
            [qn_ref[0, :, hs], jnp.where((lane // QK_ROPE) == h, qr, jnp.zeros_like(qr))], axis=1)

        km = jnp.concatenate([knm_ref[0, :, hs], krm_ref[0]], axis=1)
        s = jnp.where(metacol, _dot_nt(q, km), NEG)
        m = jnp.max(s, axis=1, keepdims=True)
        p = jnp.exp(s - m)
        l = jnp.sum(p, axis=1, keepdims=True)
        acc = _dot(p, vm_ref[0, :, hs])

        def step(j, carry, masked):
            m, l, acc = carry
            rows = pl.ds(pl.multiple_of(j * tq, tq), tq)
            kb = jnp.concatenate([kn_ref[0, rows, hs], kr_ref[0, rows, :]], axis=1)
            s = _dot_nt(q, kb)
            if masked:
                s = jnp.where(causal, s, NEG)
            mn = jnp.maximum(m, jnp.max(s, axis=1, keepdims=True))
            alpha = jnp.exp(m - mn)
            p = jnp.exp(s - mn)
            l = alpha * l + jnp.sum(p, axis=1, keepdims=True)
            acc = alpha * acc + _dot(p, v_ref[0, rows, hs])
            return mn, l, acc

        carry = lax.fori_loop(0, qi, lambda j, c: step(j, c, False), (m, l, acc))
        m, l, acc = step(qi, carry, True)
        o_ref[0, :, hs] = (acc / l).astype(o_ref.dtype)


def _attention(qn, qr, kn, kr, v, knm, krm, vm, *, batch, seq, tq, nmeta):
    di = qn.shape[1]
    npairs = di // (2 * LANES)
    r3 = lambda a: a.reshape(batch, seq, a.shape[1])
    mp = knm.shape[0]
    kern = functools.partial(_attn_kernel, tq=tq, nmeta=nmeta)
    return pl.pallas_call(
        kern,
        grid=(batch, npairs, seq // tq),
        in_specs=[
            pl.BlockSpec((1, tq, 2 * LANES), lambda b, p, i: (b, i, p)),
            pl.BlockSpec((1, tq, LANES), lambda b, p, i: (b, i, p)),
            pl.BlockSpec((1, seq, 2 * LANES), lambda b, p, i: (b, 0, p)),
            pl.BlockSpec((1, seq, LANES), lambda b, p, i: (b, 0, 0)),
            pl.BlockSpec((1, seq, 2 * LANES), lambda b, p, i: (b, 0, p)),
            pl.BlockSpec((1, mp, 2 * LANES), lambda b, p, i: (0, 0, p)),
            pl.BlockSpec((1, mp, LANES), lambda b, p, i: (0, 0, 0)),
            pl.BlockSpec((1, mp, 2 * LANES), lambda b, p, i: (0, 0, p)),
        ],
        out_specs=pl.BlockSpec((1, tq, 2 * LANES), lambda b, p, i: (b, i, p)),
        out_shape=jax.ShapeDtypeStruct((batch, seq, di), BF16),
        compiler_params=_cparams(("arbitrary", "arbitrary", "arbitrary")),
        name="mla_attention",
    )(r3(qn), r3(qr), r3(kn), r3(kr), r3(v), knm[None], krm[None], vm[None])


def _pick(n, cands):
    for c in cands:
        if n % c == 0:
            return c
    return n


def kernel(x, meta_tokens, norm_pre, norm_post, rwkv_mu, rwkv_w_in, rwkv_w0, rwkv_w2, rwkv_a0, rwkv_a2,
           rwkv_k_k, rwkv_k_a, rwkv_r_k, rwkv_ln_w, rwkv_ln_b, rwkv_w_out,
           mla_w_in, mla_q_norm, mla_w_q_up, mla_kv_norm, mla_w_kv_up, mla_w_out):
    B, T, D = x.shape
    DI = rwkv_w_out.shape[1]
    NH = DI // HEAD
    NP = DI // LANES
    MH = DI // V_HEAD
    nq = mla_q_norm.shape[1]
    nkv = mla_kv_norm.shape[1]
    MP = CHUNK
    assert meta_tokens.shape[0] == N_META and T % CHUNK == 0

    w_in = rwkv_w_in[0]
    w_main = w_in[:, :4 * DI].astype(BF16)
    w_lora = w_in[:, 4 * DI:].astype(BF16)
    mu_main = rwkv_mu[0, :4].reshape(4, 1, D)
    mu_lora = rwkv_mu[0, 4:].reshape(2, 1, D)
    zeros1 = jnp.zeros((DI,), F32)
    par = jnp.stack([rwkv_w0[0], rwkv_a0[0], rwkv_k_k[0], rwkv_k_a[0], rwkv_r_k[0],
                     rwkv_ln_w[0], rwkv_ln_b[0], zeros1]).reshape(8, NP, LANES).transpose(1, 0, 2)
    w2p = rwkv_w2[0].reshape(HEAD, NP, LANES).transpose(1, 0, 2)
    a2p = rwkv_a2[0].reshape(HEAD, NP, LANES).transpose(1, 0, 2)
    zblk = jnp.zeros_like(w2p)
    w2a = jnp.concatenate([jnp.concatenate([w2p, zblk], axis=2),
                           jnp.concatenate([zblk, a2p], axis=2)], axis=1).astype(BF16)
    w_out0 = rwkv_w_out[0].astype(BF16)

    mw = mla_w_in[0]
    kr_w = mw[:, nq + nkv:nq + nkv + QK_ROPE]
    half = QK_ROPE // 2
    rot = lambda wr: jnp.concatenate([-wr[..., half:], wr[..., :half]], axis=-1)
    win2 = jnp.concatenate([mw[:, :nq + nkv], kr_w, rot(kr_w), mw[:, nq + nkv + QK_ROPE:]],
                           axis=1).astype(BF16)
    wq = mla_w_q_up[0].reshape(nq, MH, QK_NOPE + QK_ROPE)
    wq_r = wq[:, :, QK_NOPE:]
    wq2 = jnp.concatenate([wq[:, :, :QK_NOPE].reshape(nq, MH * QK_NOPE),
                           wq_r.reshape(nq, MH * QK_ROPE),
                           rot(wq_r).reshape(nq, MH * QK_ROPE)], axis=1).astype(BF16)
    wkv = mla_w_kv_up[0].reshape(nkv, MH, QK_NOPE + V_HEAD)
    wkv2 = jnp.concatenate([wkv[:, :, :QK_NOPE].reshape(nkv, MH * QK_NOPE),
                            wkv[:, :, QK_NOPE:].reshape(nkv, MH * V_HEAD)], axis=1).astype(BF16)
    w_out1 = mla_w_out[0].astype(BF16)

    pos = jnp.arange(N_META + T, dtype=F32)
    inv_freq = jnp.exp(-math.log(ROPE_THETA) * jnp.arange(half, dtype=F32) / half)
    ang = pos[:, None] * inv_freq[None, :]
    rc = jnp.tile(jnp.cos(ang), (1, 4))
    rs = jnp.tile(jnp.sin(ang), (1, 4))
    scale = (QK_NOPE + QK_ROPE) ** -0.5

    gpre0 = norm_pre[0:1]
    gpre1 = norm_pre[1:2]
    gpost0 = norm_post[0:1]
    gpost1 = norm_post[1:2]

    xm = jnp.concatenate([jnp.zeros((MP - N_META, D), x.dtype), meta_tokens.astype(x.dtype)], axis=0)
    zero8 = jnp.zeros((8, D), x.dtype)
    hp_m, hl_m = _inproj(xm, zero8, gpre0, mu_main, mu_lora, w_main, w_lora, seq=MP, tr=MP)
    s_zero = jnp.zeros((NP, HEAD, LANES), F32)
    y_m, s_meta = _wkv(hp_m, hl_m, par, w2a, s_zero, batch=1, seq=MP, tb=MP)
    h1_m = _outproj(y_m.reshape(MP, DI), hp_m, 3, xm, w_out0, gpost0, tr=MP)
    rc_m = jnp.concatenate([jnp.zeros((MP - N_META, LANES), F32), rc[:N_META]], axis=0)
    rs_m = jnp.concatenate([jnp.zeros((MP - N_META, LANES), F32), rs[:N_META]], axis=0)
    _, _, kn_m, v_m, kr_m, _ = _mlaproj(h1_m, gpre1, win2, mla_q_norm, wq2, mla_kv_norm, wkv2,
                                        rc_m, rs_m, tr=MP, scale=scale)
    padrows = lambda a: jnp.concatenate(
        [a[MP - N_META:], jnp.zeros((LANES - N_META, a.shape[1]), a.dtype)], axis=0)
    knm, vm, krm = padrows(kn_m), padrows(v_m), padrows(kr_m)

    x2 = x.reshape(B * T, D)
    tr_in = _pick(T, (1024, 512, 256, 128, 64))
    hp, hl = _inproj(x2, meta_tokens[N_META - 8:].astype(x.dtype), gpre0, mu_main, mu_lora,
                     w_main, w_lora, seq=T, tr=tr_in)
    tb = _pick(T, (256, 128, 64))
    y, _ = _wkv(hp, hl, par, w2a, s_meta[0], batch=B, seq=T, tb=tb)
    tr_o = _pick(T, (512, 256, 128, 64))
    h1 = _outproj(y.reshape(B * T, DI), hp, 3, x2, w_out0, gpost0, tr=tr_o)

    tr_p = _pick(T, (256, 128, 64))
    rc_t = jnp.tile(rc[N_META:], (B, 1))
    rs_t = jnp.tile(rs[N_META:], (B, 1))
    qn, qr, kn, v, kr, g1 = _mlaproj(h1, gpre1, win2, mla_q_norm, wq2, mla_kv_norm, wkv2,
                                     rc_t, rs_t, tr=tr_p, scale=scale)
    tq = _pick(T, (512, 256, 128, 64))
    o = _attention(qn, qr, kn, kr, v, knm, krm, vm, batch=B, seq=T, tq=tq, nmeta=N_META)
    out = _outproj(o.reshape(B * T, DI), g1, 0, h1, w_out1, gpost1, tr=tr_o)
    return out.reshape(B, T, D)
```

```python
import functools
import math

import jax
import jax.numpy as jnp
from jax import lax
from jax.experimental import pallas as pl
from jax.experimental.pallas import tpu as pltpu

F32 = jnp.float32
BF16 = jnp.bfloat16

RMS_EPS = 1e-6
GN_EPS = 64e-5
ROPE_THETA = 10000.0
N_META = 16
HEAD = 64
LANES = 128
CHUNK = 64
QK_NOPE = 128
QK_ROPE = 64
V_HEAD = 128
VMEM_LIMIT = 56 * 1024 * 1024


def _cparams(sem):
    return pltpu.CompilerParams(dimension_semantics=sem, vmem_limit_bytes=VMEM_LIMIT)


def _rms(x, g):
    return x * lax.rsqrt(jnp.mean(x * x, axis=-1, keepdims=True) + RMS_EPS) * g


def _dot(a, b):
    return jnp.dot(a.astype(BF16), b.astype(BF16), preferred_element_type=F32)


def _dot_nt(a, b):
    return lax.dot_general(a.astype(BF16), b.astype(BF16), (((1,), (1,)), ((), ())),
                           preferred_element_type=F32)


def _dot_tn(a, b):
    return lax.dot_general(a.astype(BF16), b.astype(BF16), (((0,), (0,)), ((), ())),
                           preferred_element_type=F32)


def _inproj_kernel(x_ref, xprev_ref, x0_ref, g_ref, mu_ref, mul_ref, w_ref, wl_ref,
                   out_ref, outl_ref, u_sc, dx_sc, *, tiles_per_seq):
    i = pl.program_id(0)
    j = pl.program_id(1)

    @pl.when(j == 0)
    def _():
        g = g_ref[...]
        u = _rms(x_ref[...], g)
        first = (i % tiles_per_seq) == 0
        prow = jnp.where(first, x0_ref[7:8, :], xprev_ref[7:8, :])
        up = _rms(prow, g)
        rows = lax.broadcasted_iota(jnp.int32, u.shape, 0)
        ush = jnp.where(rows == 0, up, pltpu.roll(u, 1, 0))
        dx = ush - u
        u_sc[...] = u
        dx_sc[...] = dx
        wl = wl_ref[...]
        ow = _dot(u + mul_ref[0] * dx, wl)
        oa = _dot(u + mul_ref[1] * dx, wl)
        lane = lax.broadcasted_iota(jnp.int32, ow.shape, 1)
        outl_ref[...] = jnp.where(lane < HEAD, ow, oa).astype(outl_ref.dtype)

    xg = u_sc[...] + mu_ref[0] * dx_sc[...]
    out_ref[...] = _dot(xg, w_ref[...]).astype(out_ref.dtype)


def _inproj(x2, x0, g, mu_main, mu_lora, w_main, w_lora, *, seq, tr):
    rows, d = x2.shape
    ncol = w_main.shape[1]
    ngroups = mu_main.shape[0]
    gw = ncol // ngroups
    tpb = tr // 8
    kern = functools.partial(_inproj_kernel, tiles_per_seq=seq // tr)
    return pl.pallas_call(
        kern,
        grid=(rows // tr, ngroups),
        in_specs=[
            pl.BlockSpec((tr, d), lambda i, j: (i, 0)),
            pl.BlockSpec((8, d), lambda i, j: (jnp.maximum(i * tpb - 1, 0), 0)),
            pl.BlockSpec((8, d), lambda i, j: (0, 0)),
            pl.BlockSpec((1, d), lambda i, j: (0, 0)),
            pl.BlockSpec((1, 1, d), lambda i, j: (j, 0, 0)),
            pl.BlockSpec((2, 1, d), lambda i, j: (0, 0, 0)),
            pl.BlockSpec((d, gw), lambda i, j: (0, j)),
            pl.BlockSpec((d, LANES), lambda i, j: (0, 0)),
        ],
        out_specs=[
            pl.BlockSpec((tr, gw), lambda i, j: (i, j)),
            pl.BlockSpec((tr, LANES), lambda i, j: (i, 0)),
        ],
        out_shape=[
            jax.ShapeDtypeStruct((rows, ncol), BF16),
            jax.ShapeDtypeStruct((rows, LANES), BF16),
        ],
        scratch_shapes=[pltpu.VMEM((tr, d), F32), pltpu.VMEM((tr, d), F32)],
        compiler_params=_cparams(("arbitrary", "arbitrary")),
        name="rwkv_inproj",
    )(x2, x2, x0, g, mu_main, mu_lora, w_main, w_lora)


def _bd(x, m0):
    zero = jnp.zeros_like(x)
    return jnp.concatenate([jnp.where(m0, x, zero), jnp.where(m0, zero, x)], axis=0)


def _wkv_kernel(r_ref, k_ref, v_ref, lo_ref, par_ref, w2a_ref, s0_ref, y_ref, sout_ref, s_sc, *, npairs):
    C = CHUNK
    di = npairs * LANES

    @pl.when(pl.program_id(1) == 0)
    def _():
        s_sc[...] = s0_ref[...]

    par = par_ref[...]
    w0, a0, k_k, k_a, r_k, ln_w, ln_b = [par[n:n + 1] for n in range(7)]
    r = r_ref[0].astype(F32)
    k = k_ref[0].astype(F32)
    v = v_ref[0].astype(F32)
    lo = lo_ref[0].astype(F32)

    row2 = lax.broadcasted_iota(jnp.int32, (LANES, LANES), 0)
    lane2 = lax.broadcasted_iota(jnp.int32, (LANES, LANES), 1)
    samehead = (row2 // HEAD) == (lane2 // HEAD)
    ones_bd = jnp.where(samehead, 1.0, 0.0).astype(BF16)
    avg_bd = jnp.where(samehead, 1.0 / HEAD, 0.0).astype(BF16)
    psl = [slice(p * LANES, (p + 1) * LANES) for p in range(npairs)]

    def head_sum(x, wmat):
        st = _dot(jnp.concatenate([x[:, s_] for s_ in psl], axis=0), wmat)
        return jnp.concatenate([st[p * C:(p + 1) * C] for p in range(npairs)], axis=1)

    lane_l = lax.broadcasted_iota(jnp.int32, lo.shape, 1)
    wa = _dot(jnp.where(lane_l < HEAD, jnp.tanh(lo), lo), w2a_ref[...])
    w = -jax.nn.softplus(-(w0 + wa[:, :di])) - 0.5
    lw = -jnp.exp(w)
    a = jax.nn.sigmoid(a0 + wa[:, di:])

    kk = k * k_k
    kkn = kk / jnp.maximum(jnp.sqrt(head_sum(kk * kk, ones_bd)), 1e-12)
    km = k * (1.0 + (a - 1.0) * k_a)
    bonus = head_sum(r * km * r_k, ones_bd) * v
    bv = kkn * a

    tri = jnp.where(lax.broadcasted_iota(jnp.int32, (C, C), 0)
                    >= lax.broadcasted_iota(jnp.int32, (C, C), 1), 1.0, 0.0).astype(BF16)
    hi = lw.astype(BF16)
    lo2 = (lw - hi.astype(F32)).astype(BF16)
    cs2 = jnp.dot(tri, jnp.concatenate([hi, lo2], axis=1), preferred_element_type=F32)
    cs = cs2[:, :di] + cs2[:, di:]
    last = cs[C - 1:C]
    e_out = jnp.exp(-cs)
    e_l = jnp.exp(last - cs)
    wc = jnp.exp(last)
    rt = r * jnp.exp(cs)
    at = -kkn * jnp.exp(cs - lw)
    kt = km * e_out
    bt = bv * e_out
    kp = km * e_l
    bp = bv * e_l

    tpos = lax.broadcasted_iota(jnp.int32, (C, LANES), 0)
    lane_c = lax.broadcasted_iota(jnp.int32, (C, LANES), 1)
    spos = lane_c % HEAD
    m0 = lane_c < HEAD
    strict = tpos > spos
    incl = tpos >= spos
    eye = jnp.where(tpos == spos, 1.0, 0.0).astype(F32)
    P = range(npairs)
    cat0 = lambda x, y: jnp.concatenate([x, y], axis=0)
    cat1 = lambda x, y: jnp.concatenate([x, y], axis=1)

    sc = [_dot_nt(cat0(at[:, psl[p]], rt[:, psl[p]]), cat0(_bd(bt[:, psl[p]], m0), _bd(kt[:, psl[p]], m0)))
          for p in P]
    ab = [jnp.where(strict, sc[p][:C, :LANES], 0.0) for p in P]
    ak = [jnp.where(strict, sc[p][:C, LANES:], 0.0) for p in P]
    rb = [jnp.where(incl, sc[p][C:, :LANES], 0.0) for p in P]
    rk = [jnp.where(incl, sc[p][C:, LANES:], 0.0) for p in P]

    q = [_dot(ab[p], _bd(ab[p], m0)) for p in P]
    pm = [eye + ab[p] for p in P]
    for _ in range(4):
        pq = [_dot(cat0(pm[p], q[p]), _bd(q[p], m0)) for p in P]
        pm = [pm[p] + pq[p][:C] for p in P]
        q = [pq[p][C:] for p in P]
    pm = [pm[p] + _dot(pm[p], _bd(q[p], m0)) for p in P]

    lv = [_dot(cat0(ak[p], rk[p]), _bd(v[:, psl[p]], m0)) for p in P]
    tt = [_dot(pm[p], cat1(_bd(at[:, psl[p]], m0), _bd(lv[p][:C], m0))) for p in P]
    qy = [_dot(rb[p], cat1(_bd(tt[p][:, :LANES], m0), _bd(tt[p][:, LANES:], m0))) for p in P]
    mfull = [_dot_tn(tt[p][:, :LANES], bp[:, psl[p]]) for p in P]
    hfull = [_dot_tn(cat0(tt[p][:, LANES:], v[:, psl[p]]), cat0(bp[:, psl[p]], kp[:, psl[p]])) for p in P]

    ys = []
    for p in P:
        s = s_sc[p]
        qm = rt[:, psl[p]] + qy[p][:, :LANES]
        ys.append(qy[p][:, LANES:] + lv[p][C:] + _dot_nt(qm, _bd(s, m0)))
        s = (s * wc[:, psl[p]] + _dot(s, jnp.where(samehead, mfull[p], 0.0))
             + jnp.where(m0, hfull[p][:HEAD], hfull[p][HEAD:]))
        s_sc[p] = s
        sout_ref[0, p] = s

    y = jnp.concatenate(ys, axis=1)
    d = y - head_sum(y, avg_bd)
    var = head_sum(d * d, avg_bd)
    y_ref[0] = (d * lax.rsqrt(var + GN_EPS) * ln_w + ln_b + bonus).astype(y_ref.dtype)


def _wkv(hp, hl, par, w2a, s0, *, batch, seq):
    d_inner = hp.shape[1] // 4
    npairs = d_inner // LANES
    hp3 = hp.reshape(batch, seq, 4 * d_inner)
    hl3 = hl.reshape(batch, seq, LANES)
    kern = functools.partial(_wkv_kernel, npairs=npairs)
    return pl.pallas_call(
        kern,
        grid=(batch, seq // CHUNK),
        in_specs=[
            pl.BlockSpec((1, CHUNK, d_inner), lambda b, t: (b, t, 0)),
            pl.BlockSpec((1, CHUNK, d_inner), lambda b, t: (b, t, 1)),
            pl.BlockSpec((1, CHUNK, d_inner), lambda b, t: (b, t, 2)),
            pl.BlockSpec((1, CHUNK, LANES), lambda b, t: (b, t, 0)),
            pl.BlockSpec((8, d_inner), lambda b, t: (0, 0)),
            pl.BlockSpec((LANES, 2 * d_inner), lambda b, t: (0, 0)),
            pl.BlockSpec((npairs, HEAD, LANES), lambda b, t: (0, 0, 0)),
        ],
        out_specs=[
            pl.BlockSpec((1, CHUNK, d_inner), lambda b, t: (b, t, 0)),
            pl.BlockSpec((1, npairs, HEAD, LANES), lambda b, t: (b, 0, 0, 0)),
        ],
        out_shape=[
            jax.ShapeDtypeStruct((batch, seq, d_inner), BF16),
            jax.ShapeDtypeStruct((batch, npairs, HEAD, LANES), F32),
        ],
        scratch_shapes=[pltpu.VMEM((npairs, HEAD, LANES), F32)],
        compiler_params=_cparams(("arbitrary", "arbitrary")),
        name="wkv7",
    )(hp3, hp3, hp3, hl3, par, w2a, s0)


def _outproj_kernel(y_ref, g_ref, res_ref, w_ref, gn_ref, out_ref):
    g = g_ref[...].astype(F32)
    z = y_ref[...].astype(F32) * (g * jax.nn.sigmoid(g))
    m = _dot(z, w_ref[...])
    out_ref[...] = res_ref[...] + _rms(m, gn_ref[...])


def _outproj(y2, gsrc, gcol, res2, w, gn, *, tr):
    rows, di = y2.shape
    d = w.shape[1]
    return pl.pallas_call(
        _outproj_kernel,
        grid=(rows // tr,),
        in_specs=[
            pl.BlockSpec((tr, di), lambda i: (i, 0)),
            pl.BlockSpec((tr, di), lambda i: (i, gcol)),
            pl.BlockSpec((tr, d), lambda i: (i, 0)),
            pl.BlockSpec((di, d), lambda i: (0, 0)),
            pl.BlockSpec((1, d), lambda i: (0, 0)),
        ],
        out_specs=pl.BlockSpec((tr, d), lambda i: (i, 0)),
        out_shape=jax.ShapeDtypeStruct((rows, d), F32),
        compiler_params=_cparams(("arbitrary",)),
        name="gate_outproj",
    )(y2, gsrc, res2, w, gn)


def _mlaproj_kernel(h_ref, gpre_ref, win_ref, qn_ref, wq_ref, kvn_ref, wkv_ref, rc_ref, rs_ref,
                    oqn_ref, oqr_ref, okn_ref, ov_ref, okr_ref, og_ref, *, scale):
    u = _rms(h_ref[...], gpre_ref[...])
    t = _dot(u, win_ref[...])
    nq = qn_ref.shape[1]
    nkv = kvn_ref.shape[1]
    di = og_ref.shape[1]
    c_q = t[:, :nq]
    c_kv = t[:, nq:nq + nkv]
    krr = t[:, nq + nkv:nq + nkv + LANES]
    og_ref[...] = t[:, nq + nkv + LANES:].astype(og_ref.dtype)

    rc = rc_ref[...]
    rs = rs_ref[...]
    lane = lax.broadcasted_iota(jnp.int32, krr.shape, 1)
    prod = krr * jnp.where(lane < QK_ROPE, rc, rs)
    okr_ref[...] = (prod + pltpu.roll(prod, QK_ROPE, 1)).astype(okr_ref.dtype)

    q = _dot(_rms(c_q, qn_ref[...]), wq_ref[...]) * scale
    oqn_ref[...] = q[:, :di].astype(oqn_ref.dtype)
    nrep = (di // 2) // LANES
    rct = jnp.concatenate([rc] * nrep, axis=1)
    rst = jnp.concatenate([rs] * nrep, axis=1)
    oqr_ref[...] = (q[:, di:di + di // 2] * rct + q[:, di + di // 2:] * rst).astype(oqr_ref.dtype)

    kv = _dot(_rms(c_kv, kvn_ref[...]), wkv_ref[...])
    okn_ref[...] = kv[:, :di].astype(okn_ref.dtype)
    ov_ref[...] = kv[:, di:].astype(ov_ref.dtype)


def _mlaproj(h2, gpre, win, qn, wq, kvn, wkv, rc, rs, *, tr, scale):
    rows, d = h2.shape
    di = wkv.shape[1] // 2
    full = lambda a: pl.BlockSpec(a.shape, lambda i: (0, 0))
    rowblk = lambda n: pl.BlockSpec((tr, n), lambda i: (i, 0))
    kern = functools.partial(_mlaproj_kernel, scale=scale)
    return pl.pallas_call(
        kern,
        grid=(rows // tr,),
        in_specs=[rowblk(d), full(gpre), full(win), full(qn), full(wq), full(kvn), full(wkv),
                  rowblk(LANES), rowblk(LANES)],
        out_specs=[rowblk(di), rowblk(di // 2), rowblk(di), rowblk(di), rowblk(LANES), rowblk(di)],
        out_shape=[
            jax.ShapeDtypeStruct((rows, di), BF16),
            jax.ShapeDtypeStruct((rows, di // 2), BF16),
            jax.ShapeDtypeStruct((rows, di), BF16),
            jax.ShapeDtypeStruct((rows, di), BF16),
            jax.ShapeDtypeStruct((rows, LANES), BF16),
            jax.ShapeDtypeStruct((rows, di), BF16),
        ],
        compiler_params=_cparams(("arbitrary",)),
        name="mla_proj",
    )(h2, gpre, win, qn, wq, kvn, wkv, rc, rs)


NEG = -1e30


def _attn_kernel(qn_ref, qr_ref, kn_ref, kr_ref, v_ref, knm_ref, krm_ref, vm_ref, o_ref, *, tq, nmeta):
    qi = pl.program_id(2)
    lane = lax.broadcasted_iota(jnp.int32, (tq, LANES), 1)
    qr = qr_ref[0]
    rowp = lax.broadcasted_iota(jnp.int32, (tq, tq), 0)
    colp = lax.broadcasted_iota(jnp.int32, (tq, tq), 1)
    causal = colp <= rowp
    metacol = lax.broadcasted_iota(jnp.int32, (tq, knm_ref.shape[1]), 1) < nmeta

    for h in range(2):
        hs = slice(h * LANES, (h + 1) * LANES)
        q = jnp.concatenate(
            [qn_ref[0, :, hs], jnp.where((lane // QK_ROPE) == h, qr, jnp.zeros_like(qr))], axis=1)

        km = jnp.concatenate([knm_ref[0, :, hs], krm_ref[0]], axis=1)
        s = jnp.where(metacol, _dot_nt(q, km), NEG)
        m = jnp.max(s, axis=1, keepdims=True)
        p = jnp.exp(s - m)
        l = jnp.sum(p, axis=1, keepdims=True)
        acc = _dot(p, vm_ref[0, :, hs])

        def step(j, carry, masked):
            m, l, acc = carry
            rows = pl.ds(pl.multiple_of(j * tq, tq), tq)
            kb = jnp.concatenate([kn_ref[0, rows, hs], kr_ref[0, rows, :]], axis=1)
            s = _dot_nt(q, kb)
            if masked:
                s = jnp.where(causal, s, NEG)
            mn = jnp.maximum(m, jnp.max(s, axis=1, keepdims=True))
            alpha = jnp.exp(m - mn)
            p = jnp.exp(s - mn)
            l = alpha * l + jnp.sum(p, axis=1, keepdims=True)
            acc = alpha * acc + _dot(p, v_ref[0, rows, hs])
            return mn, l, acc

        carry = lax.fori_loop(0, qi, lambda j, c: step(j, c, False), (m, l, acc))
        m, l, acc = step(qi, carry, True)
        o_ref[0, :, hs] = (acc / l).astype(o_ref.dtype)


def _attention(qn, qr, kn, kr, v, knm, krm, vm, *, batch, seq, tq, nmeta):
    di = qn.shape[1]
    npairs = di // (2 * LANES)
    r3 = lambda a: a.reshape(batch, seq, a.shape[1])
    mp = knm.shape[0]
    kern = functools.partial(_attn_kernel, tq=tq, nmeta=nmeta)
    return pl.pallas_call(
        kern,
        grid=(batch, npairs, seq // tq),
        in_specs=[
            pl.BlockSpec((1, tq, 2 * LANES), lambda b, p, i: (b, i, p)),
            pl.BlockSpec((1, tq, LANES), lambda b, p, i: (b, i, p)),
            pl.BlockSpec((1, seq, 2 * LANES), lambda b, p, i: (b, 0, p)),
            pl.BlockSpec((1, seq, LANES), lambda b, p, i: (b, 0, 0)),
            pl.BlockSpec((1, seq, 2 * LANES), lambda b, p, i: (b, 0, p)),
            pl.BlockSpec((1, mp, 2 * LANES), lambda b, p, i: (0, 0, p)),
            pl.BlockSpec((1, mp, LANES), lambda b, p, i: (0, 0, 0)),
            pl.BlockSpec((1, mp, 2 * LANES), lambda b, p, i: (0, 0, p)),
        ],
        out_specs=pl.BlockSpec((1, tq, 2 * LANES), lambda b, p, i: (b, i, p)),
        out_shape=jax.ShapeDtypeStruct((batch, seq, di), BF16),
        compiler_params=_cparams(("arbitrary", "arbitrary", "arbitrary")),
        name="mla_attention",
    )(r3(qn), r3(qr), r3(kn), r3(kr), r3(v), knm[None], krm[None], vm[None])


def _pick(n, cands):
    for c in cands:
        if n % c == 0:
            return c
    return n


def kernel(x, meta_tokens, norm_pre, norm_post, rwkv_mu, rwkv_w_in, rwkv_w0, rwkv_w2, rwkv_a0, rwkv_a2,
           rwkv_k_k, rwkv_k_a, rwkv_r_k, rwkv_ln_w, rwkv_ln_b, rwkv_w_out,
           mla_w_in, mla_q_norm, mla_w_q_up, mla_kv_norm, mla_w_kv_up, mla_w_out):
    B, T, D = x.shape
    DI = rwkv_w_out.shape[1]
    NP = DI // LANES
    MH = DI // V_HEAD
    nq = mla_q_norm.shape[1]
    nkv = mla_kv_norm.shape[1]
    MP = CHUNK
    assert meta_tokens.shape[0] == N_META and T % CHUNK == 0

    w_in = rwkv_w_in[0]
    w_main = w_in[:, :4 * DI].astype(BF16)
    w_lora = w_in[:, 4 * DI:].astype(BF16)
    mu_main = rwkv_mu[0, :4].reshape(4, 1, D)
    mu_lora = rwkv_mu[0, 4:].reshape(2, 1, D)
    zeros1 = jnp.zeros((DI,), F32)
    par = jnp.stack([rwkv_w0[0], rwkv_a0[0], rwkv_k_k[0], rwkv_k_a[0], rwkv_r_k[0],
                     rwkv_ln_w[0], rwkv_ln_b[0], zeros1])
    zblk = jnp.zeros_like(rwkv_w2[0])
    w2a = jnp.concatenate([jnp.concatenate([rwkv_w2[0], zblk], axis=1),
                           jnp.concatenate([zblk, rwkv_a2[0]], axis=1)], axis=0).astype(BF16)
    w_out0 = rwkv_w_out[0].astype(BF16)

    mw = mla_w_in[0]
    kr_w = mw[:, nq + nkv:nq + nkv + QK_ROPE]
    half = QK_ROPE // 2
    rot = lambda wr: jnp.concatenate([-wr[..., half:], wr[..., :half]], axis=-1)
    win2 = jnp.concatenate([mw[:, :nq + nkv], kr_w, rot(kr_w), mw[:, nq + nkv + QK_ROPE:]],
                           axis=1).astype(BF16)
    wq = mla_w_q_up[0].reshape(nq, MH, QK_NOPE + QK_ROPE)
    wq_r = wq[:, :, QK_NOPE:]
    wq2 = jnp.concatenate([wq[:, :, :QK_NOPE].reshape(nq, MH * QK_NOPE),
                           wq_r.reshape(nq, MH * QK_ROPE),
                           rot(wq_r).reshape(nq, MH * QK_ROPE)], axis=1).astype(BF16)
    wkv = mla_w_kv_up[0].reshape(nkv, MH, QK_NOPE + V_HEAD)
    wkv2 = jnp.concatenate([wkv[:, :, :QK_NOPE].reshape(nkv, MH * QK_NOPE),
                            wkv[:, :, QK_NOPE:].reshape(nkv, MH * V_HEAD)], axis=1).astype(BF16)
    w_out1 = mla_w_out[0].astype(BF16)

    pos = jnp.arange(N_META + T, dtype=F32)
    inv_freq = jnp.exp(-math.log(ROPE_THETA) * jnp.arange(half, dtype=F32) / half)
    ang = pos[:, None] * inv_freq[None, :]
    rc = jnp.tile(jnp.cos(ang), (1, 4))
    rs = jnp.tile(jnp.sin(ang), (1, 4))
    scale = (QK_NOPE + QK_ROPE) ** -0.5

    gpre0 = norm_pre[0:1]
    gpre1 = norm_pre[1:2]
    gpost0 = norm_post[0:1]
    gpost1 = norm_post[1:2]

    xm = jnp.concatenate([jnp.zeros((MP - N_META, D), x.dtype), meta_tokens.astype(x.dtype)], axis=0)
    zero8 = jnp.zeros((8, D), x.dtype)
    hp_m, hl_m = _inproj(xm, zero8, gpre0, mu_main, mu_lora, w_main, w_lora, seq=MP, tr=MP)
    s_zero = jnp.zeros((NP, HEAD, LANES), F32)
    y_m, s_meta = _wkv(hp_m, hl_m, par, w2a, s_zero, batch=1, seq=MP)
    h1_m = _outproj(y_m.reshape(MP, DI), hp_m, 3, xm, w_out0, gpost0, tr=MP)
    rc_m = jnp.concatenate([jnp.zeros((MP - N_META, LANES), F32), rc[:N_META]], axis=0)
    rs_m = jnp.concatenate([jnp.zeros((MP - N_META, LANES), F32), rs[:N_META]], axis=0)
    _, _, kn_m, v_m, kr_m, _ = _mlaproj(h1_m, gpre1, win2, mla_q_norm, wq2, mla_kv_norm, wkv2,
                                        rc_m, rs_m, tr=MP, scale=scale)
    padrows = lambda a: jnp.concatenate(
        [a[MP - N_META:], jnp.zeros((LANES - N_META, a.shape[1]), a.dtype)], axis=0)
    knm, vm, krm = padrows(kn_m), padrows(v_m), padrows(kr_m)

    x2 = x.reshape(B * T, D)
    tr_in = _pick(T, (1024, 512, 256, 128, 64))
    hp, hl = _inproj(x2, meta_tokens[N_META - 8:].astype(x.dtype), gpre0, mu_main, mu_lora,
                     w_main, w_lora, seq=T, tr=tr_in)
    y, _ = _wkv(hp, hl, par, w2a, s_meta[0], batch=B, seq=T)
    tr_o = _pick(T, (512, 256, 128, 64))
    h1 = _outproj(y.reshape(B * T, DI), hp, 3, x2, w_out0, gpost0, tr=tr_o)

    tr_p = _pick(T, (256, 128, 64))
    rc_t = jnp.tile(rc[N_META:], (B, 1))
    rs_t = jnp.tile(rs[N_META:], (B, 1))
    qn, qr, kn, v, kr, g1 = _mlaproj(h1, gpre1, win2, mla_q_norm, wq2, mla_kv_norm, wkv2,
                                     rc_t, rs_t, tr=tr_p, scale=scale)
    tq = _pick(T, (512, 256, 128, 64))
    o = _attention(qn, qr, kn, kr, v, knm, krm, vm, batch=B, seq=T, tq=tq, nmeta=N_META)
    out = _outproj(o.reshape(B * T, DI), g1, 0, h1, w_out1, gpost1, tr=tr_o)
    return out.reshape(B, T, D)
```

```python
import functools
import math

import jax
import jax.numpy as jnp
from jax import lax
from jax.experimental import pallas as pl
from jax.experimental.pallas import tpu as pltpu

F32 = jnp.float32
BF16 = jnp.bfloat16

RMS_EPS = 1e-6
GN_EPS = 64e-5
ROPE_THETA = 10000.0
N_META = 16
HEAD = 64
LANES = 128
CHUNK = 64
QK_NOPE = 128
QK_ROPE = 64
V_HEAD = 128
VMEM_LIMIT = 56 * 1024 * 1024


def _cparams(sem):
    return pltpu.CompilerParams(dimension_semantics=sem, vmem_limit_bytes=VMEM_LIMIT)


def _rms(x, g):
    return x * lax.rsqrt(jnp.mean(x * x, axis=-1, keepdims=True) + RMS_EPS) * g


def _dot(a, b):
    return jnp.dot(a.astype(BF16), b.astype(BF16), preferred_element_type=F32)


def _dot_nt(a, b):
    return lax.dot_general(a.astype(BF16), b.astype(BF16), (((1,), (1,)), ((), ())),
                           preferred_element_type=F32)


def _dot_tn(a, b):
    return lax.dot_general(a.astype(BF16), b.astype(BF16), (((0,), (0,)), ((), ())),
                           preferred_element_type=F32)


def _inproj_kernel(x_ref, xprev_ref, x0_ref, g_ref, mu_ref, mul_ref, w_ref, wl_ref,
                   out_ref, outl_ref, u_sc, dx_sc, *, tiles_per_seq):
    i = pl.program_id(0)
    j = pl.program_id(1)

    @pl.when(j == 0)
    def _():
        g = g_ref[...]
        u = _rms(x_ref[...], g)
        first = (i % tiles_per_seq) == 0
        prow = jnp.where(first, x0_ref[7:8, :], xprev_ref[7:8, :])
        up = _rms(prow, g)
        rows = lax.broadcasted_iota(jnp.int32, u.shape, 0)
        ush = jnp.where(rows == 0, up, pltpu.roll(u, 1, 0))
        dx = ush - u
        u_sc[...] = u
        dx_sc[...] = dx
        wl = wl_ref[...]
        ow = _dot(u + mul_ref[0] * dx, wl)
        oa = _dot(u + mul_ref[1] * dx, wl)
        lane = lax.broadcasted_iota(jnp.int32, ow.shape, 1)
        outl_ref[...] = jnp.where(lane < HEAD, ow, oa).astype(outl_ref.dtype)

    xg = u_sc[...] + mu_ref[0] * dx_sc[...]
    out_ref[...] = _dot(xg, w_ref[...]).astype(out_ref.dtype)


def _inproj(x2, x0, g, mu_main, mu_lora, w_main, w_lora, *, seq, tr):
    rows, d = x2.shape
    ncol = w_main.shape[1]
    ngroups = mu_main.shape[0]
    gw = ncol // ngroups
    tpb = tr // 8
    kern = functools.partial(_inproj_kernel, tiles_per_seq=seq // tr)
    return pl.pallas_call(
        kern,
        grid=(rows // tr, ngroups),
        in_specs=[
            pl.BlockSpec((tr, d), lambda i, j: (i, 0)),
            pl.BlockSpec((8, d), lambda i, j: (jnp.maximum(i * tpb - 1, 0), 0)),
            pl.BlockSpec((8, d), lambda i, j: (0, 0)),
            pl.BlockSpec((1, d), lambda i, j: (0, 0)),
            pl.BlockSpec((1, 1, d), lambda i, j: (j, 0, 0)),
            pl.BlockSpec((2, 1, d), lambda i, j: (0, 0, 0)),
            pl.BlockSpec((d, gw), lambda i, j: (0, j)),
            pl.BlockSpec((d, LANES), lambda i, j: (0, 0)),
        ],
        out_specs=[
            pl.BlockSpec((tr, gw), lambda i, j: (i, j)),
            pl.BlockSpec((tr, LANES), lambda i, j: (i, 0)),
        ],
        out_shape=[
            jax.ShapeDtypeStruct((rows, ncol), BF16),
            jax.ShapeDtypeStruct((rows, LANES), BF16),
        ],
        scratch_shapes=[pltpu.VMEM((tr, d), F32), pltpu.VMEM((tr, d), F32)],
        compiler_params=_cparams(("arbitrary", "arbitrary")),
        name="rwkv_inproj",
    )(x2, x2, x0, g, mu_main, mu_lora, w_main, w_lora)


def _bd(x, m0):
    zero = jnp.zeros_like(x)
    return jnp.concatenate([jnp.where(m0, x, zero), jnp.where(m0, zero, x)], axis=0)


def _wkv_kernel(r_ref, k_ref, v_ref, lo_ref, par_ref, w2a_ref, s0_ref, y_ref, sout_ref, s_sc, *, npairs):
    C = CHUNK
    di = npairs * LANES

    @pl.when(pl.program_id(1) == 0)
    def _():
        s_sc[...] = s0_ref[...]

    par = par_ref[...]
    w0, a0, k_k, k_a, r_k, ln_w, ln_b = [par[n:n + 1] for n in range(7)]
    r = r_ref[0].astype(F32)
    k = k_ref[0].astype(F32)
    v = v_ref[0].astype(F32)
    lo = lo_ref[0].astype(F32)

    row2 = lax.broadcasted_iota(jnp.int32, (LANES, LANES), 0)
    lane2 = lax.broadcasted_iota(jnp.int32, (LANES, LANES), 1)
    samehead = (row2 // HEAD) == (lane2 // HEAD)
    ones_bd = jnp.where(samehead, 1.0, 0.0).astype(BF16)
    avg_bd = jnp.where(samehead, 1.0 / HEAD, 0.0).astype(BF16)
    psl = [slice(p * LANES, (p + 1) * LANES) for p in range(npairs)]

    def head_sum(x, wmat):
        st = _dot(jnp.concatenate([x[:, s_] for s_ in psl], axis=0), wmat)
        return jnp.concatenate([st[p * C:(p + 1) * C] for p in range(npairs)], axis=1)

    lane_l = lax.broadcasted_iota(jnp.int32, lo.shape, 1)
    wa = _dot(jnp.where(lane_l < HEAD, jnp.tanh(lo), lo), w2a_ref[...])
    lw = -math.exp(-0.5) * jax.nn.sigmoid(w0 + wa[:, :di])
    a = jax.nn.sigmoid(a0 + wa[:, di:])

    kk = k * k_k
    kkn = kk * lax.rsqrt(jnp.maximum(head_sum(kk * kk, ones_bd), 1e-24))
    km = k * (1.0 + (a - 1.0) * k_a)
    bonus = head_sum(r * km * r_k, ones_bd) * v
    bv = kkn * a

    tri = jnp.where(lax.broadcasted_iota(jnp.int32, (C, C), 0)
                    >= lax.broadcasted_iota(jnp.int32, (C, C), 1), 1.0, 0.0).astype(BF16)
    hi = lw.astype(BF16)
    lo2 = (lw - hi.astype(F32)).astype(BF16)
    cs2 = jnp.dot(tri, jnp.concatenate([hi, lo2], axis=1), preferred_element_type=F32)
    cs = cs2[:, :di] + cs2[:, di:]
    last = cs[C - 1:C]
    e_out = jnp.exp(-cs)
    e_l = jnp.exp(last - cs)
    wc = jnp.exp(last)
    rt = r * jnp.exp(cs)
    vb = v.astype(BF16)
    rtb = rt.astype(BF16)
    atb = (-kkn * jnp.exp(cs - lw)).astype(BF16)
    ktb = (km * e_out).astype(BF16)
    btb = (bv * e_out).astype(BF16)
    kpb = (km * e_l).astype(BF16)
    bpb = (bv * e_l).astype(BF16)

    tpos = lax.broadcasted_iota(jnp.int32, (C, LANES), 0)
    lane_c = lax.broadcasted_iota(jnp.int32, (C, LANES), 1)
    spos = lane_c % HEAD
    m0 = lane_c < HEAD
    strict = tpos > spos
    incl = tpos >= spos
    eye = jnp.where(tpos == spos, 1.0, 0.0).astype(F32)
    P = range(npairs)
    cat0 = lambda x, y: jnp.concatenate([x, y], axis=0)
    cat1 = lambda x, y: jnp.concatenate([x, y], axis=1)
    b16 = lambda x: x.astype(BF16)

    sc = [_dot_nt(cat0(atb[:, psl[p]], rtb[:, psl[p]]), cat0(_bd(btb[:, psl[p]], m0), _bd(ktb[:, psl[p]], m0)))
          for p in P]
    ab = [jnp.where(strict, sc[p][:C, :LANES], 0.0) for p in P]
    akrk = [b16(cat0(jnp.where(strict, sc[p][:C, LANES:], 0.0), jnp.where(incl, sc[p][C:, LANES:], 0.0)))
            for p in P]
    rbb = [b16(jnp.where(incl, sc[p][C:, :LANES], 0.0)) for p in P]

    qb = [b16(ab[p]) for p in P]
    pm = [eye + ab[p] for p in P]
    q = [_dot(qb[p], _bd(qb[p], m0)) for p in P]
    for _ in range(4):
        qb = [b16(q[p]) for p in P]
        pq = [_dot(cat0(b16(pm[p]), qb[p]), _bd(qb[p], m0)) for p in P]
        pm = [pm[p] + pq[p][:C] for p in P]
        q = [pq[p][C:] for p in P]
    pm = [pm[p] + _dot(pm[p], _bd(b16(q[p]), m0)) for p in P]

    lv = [_dot(akrk[p], _bd(vb[:, psl[p]], m0)) for p in P]
    tt = [_dot(pm[p], cat1(_bd(atb[:, psl[p]], m0), _bd(b16(lv[p][:C]), m0))) for p in P]
    ttb = [b16(tt[p]) for p in P]
    qy = [_dot(rbb[p], cat1(_bd(ttb[p][:, :LANES], m0), _bd(ttb[p][:, LANES:], m0))) for p in P]
    mfull = [_dot_tn(ttb[p][:, :LANES], bpb[:, psl[p]]) for p in P]
    hfull = [_dot_tn(cat0(ttb[p][:, LANES:], vb[:, psl[p]]), cat0(bpb[:, psl[p]], kpb[:, psl[p]])) for p in P]

    ys = []
    for p in P:
        s = s_sc[p]
        sb = b16(s)
        qm = rt[:, psl[p]] + qy[p][:, :LANES]
        ys.append(qy[p][:, LANES:] + lv[p][C:] + _dot_nt(qm, _bd(sb, m0)))
        s = (s * wc[:, psl[p]] + _dot(sb, jnp.where(samehead, mfull[p], 0.0))
             + jnp.where(m0, hfull[p][:HEAD], hfull[p][HEAD:]))
        s_sc[p] = s
        sout_ref[0, p] = s

    y = jnp.concatenate(ys, axis=1)
    d = y - head_sum(y, avg_bd)
    var = head_sum(d * d, avg_bd)
    y_ref[0] = (d * lax.rsqrt(var + GN_EPS) * ln_w + ln_b + bonus).astype(y_ref.dtype)


def _wkv(hp, hl, par, w2a, s0, *, batch, seq):
    d_inner = hp.shape[1] // 4
    npairs = d_inner // LANES
    hp3 = hp.reshape(batch, seq, 4 * d_inner)
    hl3 = hl.reshape(batch, seq, LANES)
    kern = functools.partial(_wkv_kernel, npairs=npairs)
    return pl.pallas_call(
        kern,
        grid=(batch, seq // CHUNK),
        in_specs=[
            pl.BlockSpec((1, CHUNK, d_inner), lambda b, t: (b, t, 0)),
            pl.BlockSpec((1, CHUNK, d_inner), lambda b, t: (b, t, 1)),
            pl.BlockSpec((1, CHUNK, d_inner), lambda b, t: (b, t, 2)),
            pl.BlockSpec((1, CHUNK, LANES), lambda b, t: (b, t, 0)),
            pl.BlockSpec((8, d_inner), lambda b, t: (0, 0)),
            pl.BlockSpec((LANES, 2 * d_inner), lambda b, t: (0, 0)),
            pl.BlockSpec((npairs, HEAD, LANES), lambda b, t: (0, 0, 0)),
        ],
        out_specs=[
            pl.BlockSpec((1, CHUNK, d_inner), lambda b, t: (b, t, 0)),
            pl.BlockSpec((1, npairs, HEAD, LANES), lambda b, t: (b, 0, 0, 0)),
        ],
        out_shape=[
            jax.ShapeDtypeStruct((batch, seq, d_inner), BF16),
            jax.ShapeDtypeStruct((batch, npairs, HEAD, LANES), F32),
        ],
        scratch_shapes=[pltpu.VMEM((npairs, HEAD, LANES), F32)],
        compiler_params=_cparams(("arbitrary", "arbitrary")),
        name="wkv7",
    )(hp3, hp3, hp3, hl3, par, w2a, s0)


def _outproj_kernel(y_ref, g_ref, res_ref, w_ref, gn_ref, out_ref):
    g = g_ref[...].astype(F32)
    z = y_ref[...].astype(F32) * (g * jax.nn.sigmoid(g))
    m = _dot(z, w_ref[...])
    out_ref[...] = res_ref[...] + _rms(m, gn_ref[...])


def _outproj(y2, gsrc, gcol, res2, w, gn, *, tr):
    rows, di = y2.shape
    d = w.shape[1]
    return pl.pallas_call(
        _outproj_kernel,
        grid=(rows // tr,),
        in_specs=[
            pl.BlockSpec((tr, di), lambda i: (i, 0)),
            pl.BlockSpec((tr, di), lambda i: (i, gcol)),
            pl.BlockSpec((tr, d), lambda i: (i, 0)),
            pl.BlockSpec((di, d), lambda i: (0, 0)),
            pl.BlockSpec((1, d), lambda i: (0, 0)),
        ],
        out_specs=pl.BlockSpec((tr, d), lambda i: (i, 0)),
        out_shape=jax.ShapeDtypeStruct((rows, d), F32),
        compiler_params=_cparams(("arbitrary",)),
        name="gate_outproj",
    )(y2, gsrc, res2, w, gn)


def _mlaproj_kernel(h_ref, gpre_ref, win_ref, qn_ref, wq_ref, kvn_ref, wkn_ref, wvt_ref, rc_ref, rs_ref,
                    oqn_ref, oqr_ref, okn_ref, ovt_ref, okr_ref, og_ref, *, scale):
    u = _rms(h_ref[...], gpre_ref[...])
    t = _dot(u, win_ref[...])
    nq = qn_ref.shape[1]
    nkv = kvn_ref.shape[1]
    di = og_ref.shape[1]
    c_q = t[:, :nq]
    c_kv = t[:, nq:nq + nkv]
    krr = t[:, nq + nkv:nq + nkv + LANES]
    og_ref[...] = t[:, nq + nkv + LANES:].astype(og_ref.dtype)

    rc = rc_ref[...]
    rs = rs_ref[...]
    lane = lax.broadcasted_iota(jnp.int32, krr.shape, 1)
    prod = krr * jnp.where(lane < QK_ROPE, rc, rs)
    okr_ref[...] = (prod + pltpu.roll(prod, QK_ROPE, 1)).astype(okr_ref.dtype)

    q = _dot(_rms(c_q, qn_ref[...]), wq_ref[...]) * scale
    oqn_ref[...] = q[:, :di].astype(oqn_ref.dtype)
    nrep = (di // 2) // LANES
    rct = jnp.concatenate([rc] * nrep, axis=1)
    rst = jnp.concatenate([rs] * nrep, axis=1)
    oqr_ref[...] = (q[:, di:di + di // 2] * rct + q[:, di + di // 2:] * rst).astype(oqr_ref.dtype)

    ckv = _rms(c_kv, kvn_ref[...])
    okn_ref[...] = _dot(ckv, wkn_ref[...]).astype(okn_ref.dtype)
    ovt_ref[...] = _dot_nt(wvt_ref[...], ckv).astype(ovt_ref.dtype)


def _mlaproj(h2, gpre, win, qn, wq, kvn, wkn, wvt, rc, rs, *, tr, scale):
    rows, d = h2.shape
    di = wkn.shape[1]
    full = lambda a: pl.BlockSpec(a.shape, lambda i: (0, 0))
    rowblk = lambda n: pl.BlockSpec((tr, n), lambda i: (i, 0))
    kern = functools.partial(_mlaproj_kernel, scale=scale)
    return pl.pallas_call(
        kern,
        grid=(rows // tr,),
        in_specs=[rowblk(d), full(gpre), full(win), full(qn), full(wq), full(kvn), full(wkn), full(wvt),
                  rowblk(LANES), rowblk(LANES)],
        out_specs=[rowblk(di), rowblk(di // 2), rowblk(di), pl.BlockSpec((di, tr), lambda i: (0, i)),
                   rowblk(LANES), rowblk(di)],
        out_shape=[
            jax.ShapeDtypeStruct((rows, di), BF16),
            jax.ShapeDtypeStruct((rows, di // 2), BF16),
            jax.ShapeDtypeStruct((rows, di), BF16),
            jax.ShapeDtypeStruct((di, rows), BF16),
            jax.ShapeDtypeStruct((rows, LANES), BF16),
            jax.ShapeDtypeStruct((rows, di), BF16),
        ],
        compiler_params=_cparams(("arbitrary",)),
        name="mla_proj",
    )(h2, gpre, win, qn, wq, kvn, wkn, wvt, rc, rs)


NEG = -1e30


def _attn_kernel(qn_ref, qr_ref, kn_ref, kr_ref, vt_ref, knm_ref, krm_ref, vtm_ref, o_ref, s_sc, *, tq, nmeta):
    qi = pl.program_id(2)
    mp = knm_ref.shape[1]
    lane = lax.broadcasted_iota(jnp.int32, (tq, LANES), 1)
    qr = qr_ref[0]
    hsl = [slice(h * LANES, (h + 1) * LANES) for h in range(2)]
    qs = [jnp.concatenate([qn_ref[0, :, hsl[h]],
                           jnp.where((lane // QK_ROPE) == h, qr, jnp.zeros_like(qr))], axis=1) for h in range(2)]

    def scores(j, h):
        rows = pl.ds(pl.multiple_of(j * tq, tq), tq)
        return _dot_nt(jnp.concatenate([kn_ref[0, rows, hsl[h]], kr_ref[0, rows, :]], axis=1), qs[h])

    def update(carry, s, smax, vt):
        m, l, acc = carry
        mn = jnp.maximum(m, smax)
        alpha = jnp.exp2(m - mn)
        p = jnp.exp2(s - mn)
        return [mn, alpha * l + jnp.sum(p, axis=0, keepdims=True), alpha * acc + _dot(vt, p)]

    colmax = lambda s: jnp.max(s, axis=0, keepdims=True)

    bmax = []
    for h in range(2):
        s0 = scores(0, h)
        s_sc[h] = s0
        bmax.append(colmax(s0))

    def step(j, carry):
        nxt = [scores(j + 1, h) for h in range(2)]
        nmax = [colmax(n) for n in nxt]
        rows = pl.ds(pl.multiple_of(j * tq, tq), tq)
        out = []
        for h in range(2):
            out += update(carry[4 * h:4 * h + 3], s_sc[h], carry[4 * h + 3], vt_ref[0, hsl[h], rows])
            out.append(nmax[h])
        for h in range(2):
            s_sc[h] = nxt[h]
        return out

    row1 = lambda val: jnp.full((1, tq), val, F32)
    carry = []
    for h in range(2):
        carry += [row1(NEG), row1(0.0), jnp.zeros((LANES, tq), F32), bmax[h]]
    carry = lax.fori_loop(0, qi, step, carry)

    krow = lax.broadcasted_iota(jnp.int32, (mp + tq, tq), 0)
    qcol = lax.broadcasted_iota(jnp.int32, (mp + tq, tq), 1)
    valid = (krow < nmeta) | ((krow >= mp) & ((krow - mp) <= qcol))
    drows = pl.ds(pl.multiple_of(qi * tq, tq), tq)
    sm = [_dot_nt(jnp.concatenate([knm_ref[0, :, hsl[h]], krm_ref[0]], axis=1), qs[h]) for h in range(2)]
    for h in range(2):
        s = jnp.where(valid, jnp.concatenate([sm[h], s_sc[h]], axis=0), NEG)
        vt = jnp.concatenate([vtm_ref[0, hsl[h], :], vt_ref[0, hsl[h], drows]], axis=1)
        m, l, acc = update(carry[4 * h:4 * h + 3], s, colmax(s), vt)
        o_ref[0, :, hsl[h]] = (acc / l).T.astype(o_ref.dtype)


def _attention(qn, qr, kn, kr, vt, knm, krm, vtm, *, batch, seq, tq, nmeta):
    di = qn.shape[1]
    npairs = di // (2 * LANES)
    r3 = lambda a: a.reshape(batch, seq, a.shape[1])
    mp = knm.shape[0]
    kern = functools.partial(_attn_kernel, tq=tq, nmeta=nmeta)
    return pl.pallas_call(
        kern,
        grid=(batch, npairs, seq // tq),
        in_specs=[
            pl.BlockSpec((1, tq, 2 * LANES), lambda b, p, i: (b, i, p)),
            pl.BlockSpec((1, tq, LANES), lambda b, p, i: (b, i, p)),
            pl.BlockSpec((1, seq, 2 * LANES), lambda b, p, i: (b, 0, p)),
            pl.BlockSpec((1, seq, LANES), lambda b, p, i: (b, 0, 0)),
            pl.BlockSpec((1, 2 * LANES, seq), lambda b, p, i: (0, p, b)),
            pl.BlockSpec((1, mp, 2 * LANES), lambda b, p, i: (0, 0, p)),
            pl.BlockSpec((1, mp, LANES), lambda b, p, i: (0, 0, 0)),
            pl.BlockSpec((1, 2 * LANES, mp), lambda b, p, i: (0, p, 0)),
        ],
        out_specs=pl.BlockSpec((1, tq, 2 * LANES), lambda b, p, i: (b, i, p)),
        out_shape=jax.ShapeDtypeStruct((batch, seq, di), BF16),
        scratch_shapes=[pltpu.VMEM((2, tq, tq), F32)],
        compiler_params=_cparams(("arbitrary", "arbitrary", "arbitrary")),
        name="mla_attention",
    )(r3(qn), r3(qr), r3(kn), r3(kr), vt[None], knm[None], krm[None], vtm[None])


def _pick(n, cands):
    for c in cands:
        if n % c == 0:
            return c
    return n


def kernel(x, meta_tokens, norm_pre, norm_post, rwkv_mu, rwkv_w_in, rwkv_w0, rwkv_w2, rwkv_a0, rwkv_a2,
           rwkv_k_k, rwkv_k_a, rwkv_r_k, rwkv_ln_w, rwkv_ln_b, rwkv_w_out,
           mla_w_in, mla_q_norm, mla_w_q_up, mla_kv_norm, mla_w_kv_up, mla_w_out):
    B, T, D = x.shape
    DI = rwkv_w_out.shape[1]
    NP = DI // LANES
    MH = DI // V_HEAD
    nq = mla_q_norm.shape[1]
    nkv = mla_kv_norm.shape[1]
    MP = CHUNK
    assert meta_tokens.shape[0] == N_META and T % CHUNK == 0

    w_in = rwkv_w_in[0]
    w_main = w_in[:, :4 * DI].astype(BF16)
    w_lora = w_in[:, 4 * DI:].astype(BF16)
    mu_main = rwkv_mu[0, :4].reshape(4, 1, D)
    mu_lora = rwkv_mu[0, 4:].reshape(2, 1, D)
    zeros1 = jnp.zeros((DI,), F32)
    par = jnp.stack([rwkv_w0[0], rwkv_a0[0], rwkv_k_k[0], rwkv_k_a[0], rwkv_r_k[0],
                     rwkv_ln_w[0], rwkv_ln_b[0], zeros1])
    zblk = jnp.zeros_like(rwkv_w2[0])
    w2a = jnp.concatenate([jnp.concatenate([rwkv_w2[0], zblk], axis=1),
                           jnp.concatenate([zblk, rwkv_a2[0]], axis=1)], axis=0).astype(BF16)
    w_out0 = rwkv_w_out[0].astype(BF16)

    mw = mla_w_in[0]
    kr_w = mw[:, nq + nkv:nq + nkv + QK_ROPE]
    half = QK_ROPE // 2
    rot = lambda wr: jnp.concatenate([-wr[..., half:], wr[..., :half]], axis=-1)
    win2 = jnp.concatenate([mw[:, :nq + nkv], kr_w, rot(kr_w), mw[:, nq + nkv + QK_ROPE:]],
                           axis=1).astype(BF16)
    wq = mla_w_q_up[0].reshape(nq, MH, QK_NOPE + QK_ROPE)
    wq_r = wq[:, :, QK_NOPE:]
    wq2 = jnp.concatenate([wq[:, :, :QK_NOPE].reshape(nq, MH * QK_NOPE),
                           wq_r.reshape(nq, MH * QK_ROPE),
                           rot(wq_r).reshape(nq, MH * QK_ROPE)], axis=1).astype(BF16)
    wkv = mla_w_kv_up[0].reshape(nkv, MH, QK_NOPE + V_HEAD)
    wkn = wkv[:, :, :QK_NOPE].reshape(nkv, MH * QK_NOPE).astype(BF16)
    wvt = wkv[:, :, QK_NOPE:].reshape(nkv, MH * V_HEAD).T.astype(BF16)
    w_out1 = mla_w_out[0].astype(BF16)

    pos = jnp.arange(N_META + T, dtype=F32)
    inv_freq = jnp.exp(-math.log(ROPE_THETA) * jnp.arange(half, dtype=F32) / half)
    ang = pos[:, None] * inv_freq[None, :]
    rc = jnp.tile(jnp.cos(ang), (1, 4))
    rs = jnp.tile(jnp.sin(ang), (1, 4))
    scale = (QK_NOPE + QK_ROPE) ** -0.5 * math.log2(math.e)

    gpre0 = norm_pre[0:1]
    gpre1 = norm_pre[1:2]
    gpost0 = norm_post[0:1]
    gpost1 = norm_post[1:2]

    xm = jnp.concatenate([jnp.zeros((MP - N_META, D), x.dtype), meta_tokens.astype(x.dtype)], axis=0)
    zero8 = jnp.zeros((8, D), x.dtype)
    hp_m, hl_m = _inproj(xm, zero8, gpre0, mu_main, mu_lora, w_main, w_lora, seq=MP, tr=MP)
    s_zero = jnp.zeros((NP, HEAD, LANES), F32)
    y_m, s_meta = _wkv(hp_m, hl_m, par, w2a, s_zero, batch=1, seq=MP)
    h1_m = _outproj(y_m.reshape(MP, DI), hp_m, 3, xm, w_out0, gpost0, tr=MP)
    rc_m = jnp.concatenate([jnp.zeros((MP - N_META, LANES), F32), rc[:N_META]], axis=0)
    rs_m = jnp.concatenate([jnp.zeros((MP - N_META, LANES), F32), rs[:N_META]], axis=0)
    _, _, kn_m, vt_m, kr_m, _ = _mlaproj(h1_m, gpre1, win2, mla_q_norm, wq2, mla_kv_norm, wkn, wvt,
                                         rc_m, rs_m, tr=MP, scale=scale)
    padrows = lambda a: jnp.concatenate(
        [a[MP - N_META:], jnp.zeros((LANES - N_META, a.shape[1]), a.dtype)], axis=0)
    knm, krm = padrows(kn_m), padrows(kr_m)
    vtm = jnp.concatenate([vt_m[:, MP - N_META:], jnp.zeros((DI, LANES - N_META), vt_m.dtype)], axis=1)

    x2 = x.reshape(B * T, D)
    tr_in = _pick(T, (1024, 512, 256, 128, 64))
    hp, hl = _inproj(x2, meta_tokens[N_META - 8:].astype(x.dtype), gpre0, mu_main, mu_lora,
                     w_main, w_lora, seq=T, tr=tr_in)
    y, _ = _wkv(hp, hl, par, w2a, s_meta[0], batch=B, seq=T)
    tr_o = _pick(T, (512, 256, 128, 64))
    h1 = _outproj(y.reshape(B * T, DI), hp, 3, x2, w_out0, gpost0, tr=tr_o)

    tr_p = _pick(T, (256, 128, 64))
    rc_t = jnp.tile(rc[N_META:], (B, 1))
    rs_t = jnp.tile(rs[N_META:], (B, 1))
    qn, qr, kn, vt, kr, g1 = _mlaproj(h1, gpre1, win2, mla_q_norm, wq2, mla_kv_norm, wkn, wvt,
                                      rc_t, rs_t, tr=tr_p, scale=scale)
    tq = _pick(T, (512, 256, 128))
    o = _attention(qn, qr, kn, kr, vt, knm, krm, vtm, batch=B, seq=T, tq=tq, nmeta=N_META)
    out = _outproj(o.reshape(B * T, DI), g1, 0, h1, w_out1, gpost1, tr=tr_o)
    return out.reshape(B, T, D)
```

```python
import functools
import math

import jax
import jax.numpy as jnp
from jax import lax
from jax.experimental import pallas as pl
from jax.experimental.pallas import tpu as pltpu

F32 = jnp.float32
BF16 = jnp.bfloat16

RMS_EPS = 1e-6
GN_EPS = 64e-5
ROPE_THETA = 10000.0
N_META = 16
HEAD = 64
LANES = 128
CHUNK = 64
QK_NOPE = 128
QK_ROPE = 64
V_HEAD = 128
VMEM_LIMIT = 56 * 1024 * 1024


def _cparams(sem):
    return pltpu.CompilerParams(dimension_semantics=sem, vmem_limit_bytes=VMEM_LIMIT)


def _rms(x, g):
    return x * lax.rsqrt(jnp.mean(x * x, axis=-1, keepdims=True) + RMS_EPS) * g


def _dot(a, b):
    return jnp.dot(a.astype(BF16), b.astype(BF16), preferred_element_type=F32)


def _dot_nt(a, b):
    return lax.dot_general(a.astype(BF16), b.astype(BF16), (((1,), (1,)), ((), ())),
                           preferred_element_type=F32)


def _dot_tn(a, b):
    return lax.dot_general(a.astype(BF16), b.astype(BF16), (((0,), (0,)), ((), ())),
                           preferred_element_type=F32)


def _inproj_kernel(x_ref, xprev_ref, x0_ref, g_ref, mu_ref, mul_ref, w_ref, wl_ref,
                   out_ref, outl_ref, u_sc, dx_sc, *, tiles_per_seq):
    i = pl.program_id(0)
    j = pl.program_id(1)

    @pl.when(j == 0)
    def _():
        g = g_ref[...]
        u = _rms(x_ref[...], g)
        first = (i % tiles_per_seq) == 0
        prow = jnp.where(first, x0_ref[7:8, :], xprev_ref[7:8, :])
        up = _rms(prow, g)
        rows = lax.broadcasted_iota(jnp.int32, u.shape, 0)
        ush = jnp.where(rows == 0, up, pltpu.roll(u, 1, 0))
        dx = ush - u
        u_sc[...] = u
        dx_sc[...] = dx
        wl = wl_ref[...]
        ow = _dot(u + mul_ref[0] * dx, wl)
        oa = _dot(u + mul_ref[1] * dx, wl)
        lane = lax.broadcasted_iota(jnp.int32, ow.shape, 1)
        outl_ref[...] = jnp.where(lane < HEAD, ow, oa).astype(outl_ref.dtype)

    xg = u_sc[...] + mu_ref[0] * dx_sc[...]
    out_ref[...] = _dot(xg, w_ref[...]).astype(out_ref.dtype)


def _inproj(x2, x0, g, mu_main, mu_lora, w_main, w_lora, *, seq, tr):
    rows, d = x2.shape
    ncol = w_main.shape[1]
    ngroups = mu_main.shape[0]
    gw = ncol // ngroups
    tpb = tr // 8
    kern = functools.partial(_inproj_kernel, tiles_per_seq=seq // tr)
    return pl.pallas_call(
        kern,
        grid=(rows // tr, ngroups),
        in_specs=[
            pl.BlockSpec((tr, d), lambda i, j: (i, 0)),
            pl.BlockSpec((8, d), lambda i, j: (jnp.maximum(i * tpb - 1, 0), 0)),
            pl.BlockSpec((8, d), lambda i, j: (0, 0)),
            pl.BlockSpec((1, d), lambda i, j: (0, 0)),
            pl.BlockSpec((1, 1, d), lambda i, j: (j, 0, 0)),
            pl.BlockSpec((2, 1, d), lambda i, j: (0, 0, 0)),
            pl.BlockSpec((d, gw), lambda i, j: (0, j)),
            pl.BlockSpec((d, LANES), lambda i, j: (0, 0)),
        ],
        out_specs=[
            pl.BlockSpec((tr, gw), lambda i, j: (i, j)),
            pl.BlockSpec((tr, LANES), lambda i, j: (i, 0)),
        ],
        out_shape=[
            jax.ShapeDtypeStruct((rows, ncol), BF16),
            jax.ShapeDtypeStruct((rows, LANES), BF16),
        ],
        scratch_shapes=[pltpu.VMEM((tr, d), F32), pltpu.VMEM((tr, d), F32)],
        compiler_params=_cparams(("arbitrary", "arbitrary")),
        name="rwkv_inproj",
    )(x2, x2, x0, g, mu_main, mu_lora, w_main, w_lora)


def _bd(x, m0):
    zero = jnp.zeros_like(x)
    return jnp.concatenate([jnp.where(m0, x, zero), jnp.where(m0, zero, x)], axis=0)


def _wkv_kernel(r_ref, k_ref, v_ref, lo_ref, par_ref, w2a_ref, s0_ref, y_ref, sout_ref, s_sc, *, npairs):
    C = CHUNK
    di = npairs * LANES
    nch = r_ref.shape[1] // C

    @pl.when(pl.program_id(1) == 0)
    def _():
        s_sc[...] = s0_ref[...]

    row2 = lax.broadcasted_iota(jnp.int32, (LANES, LANES), 0)
    lane2 = lax.broadcasted_iota(jnp.int32, (LANES, LANES), 1)
    samehead = (row2 // HEAD) == (lane2 // HEAD)
    ones_bd = jnp.where(samehead, 1.0, 0.0).astype(BF16)
    avg_bd = jnp.where(samehead, 1.0 / HEAD, 0.0).astype(BF16)
    P = range(npairs)

    tri =jnp.where(lax.broadcasted_iota(jnp.int32, (C, C), 0)
                    >= lax.broadcasted_iota(jnp.int32, (C, C), 1), 1.0, 0.0).astype(BF16)
    tpos = lax.broadcasted_iota(jnp.int32, (C, LANES), 0)
    lane_c = lax.broadcasted_iota(jnp.int32, (C, LANES), 1)
    spos = lane_c % HEAD
    m0 = lane_c < HEAD
    strict = tpos > spos
    incl = tpos >= spos
    eye = jnp.where(tpos == spos, 1.0, 0.0).astype(F32)
    cat0 = lambda x, y: jnp.concatenate([x, y], axis=0)
    cat1 = lambda x, y: jnp.concatenate([x, y], axis=1)
    b16 = lambda x: x.astype(BF16)

    def head_sum(xs, wmat):
        st = _dot(jnp.concatenate(xs, axis=0), wmat)
        return [st[i * C:(i + 1) * C] for i in range(len(xs))]

    def prep(c, pairs, out):
        rows = slice(c * C, (c + 1) * C)
        psl = {p: slice(p * LANES, (p + 1) * LANES) for p in pairs}
        par = lambda n, p: par_ref[n:n + 1, psl[p]]
        lo = lo_ref[0, rows, :].astype(F32)
        tl = b16(jnp.where(lane_c < HEAD, jnp.tanh(lo), lo))
        wa = {p: _dot(tl, cat1(w2a_ref[:, psl[p]], w2a_ref[:, di + p * LANES:di + (p + 1) * LANES])) for p in pairs}
        k = {p: k_ref[0, rows, psl[p]].astype(F32) for p in pairs}
        kk = {p: k[p] * par(2, p) for p in pairs}
        n2 = dict(zip(pairs, head_sum([kk[p] * kk[p] for p in pairs], ones_bd)))
        yield
        lw = {p: -math.exp(-0.5) * jax.nn.sigmoid(par(0, p) + wa[p][:, :LANES]) for p in pairs}
        a = {p: jax.nn.sigmoid(par(1, p) + wa[p][:, LANES:]) for p in pairs}
        hi = {p: b16(lw[p]) for p in pairs}
        cs2 = {p: jnp.dot(tri, cat1(hi[p], b16(lw[p] - hi[p].astype(F32))), preferred_element_type=F32)
               for p in pairs}
        yield
        r = {p: r_ref[0, rows, psl[p]].astype(F32) for p in pairs}
        v = {p: v_ref[0, rows, psl[p]].astype(F32) for p in pairs}
        km = {p: k[p] * (1.0 + (a[p] - 1.0) * par(3, p)) for p in pairs}
        bsum = dict(zip(pairs, head_sum([r[p] * km[p] * par(4, p) for p in pairs], ones_bd)))
        yield
        for p in pairs:
            kkn = kk[p] * lax.rsqrt(jnp.maximum(n2[p], 1e-24))
            bv = kkn * a[p]
            cs = cs2[p][:, :LANES] + cs2[p][:, LANES:]
            last = cs[C - 1:C]
            e_out = jnp.exp(-cs)
            e_l = jnp.exp(last - cs)
            rt = r[p] * jnp.exp(cs)
            out[p] = dict(
                wc=jnp.exp(last), bonus=bsum[p] * v[p],
                vb=b16(v[p]), rtb=b16(rt), atb=b16(-kkn * jnp.exp(cs - lw[p])), ktb=b16(km[p] * e_out),
                btb=b16(bv * e_out), kpb=b16(km[p] * e_l), bpb=b16(bv * e_l))
        yield

    def chunk(c, d, state):
        sc = [_dot(cat0(d[p]["atb"], d[p]["rtb"]), cat1(_bd(d[p]["btb"], m0).T, _bd(d[p]["ktb"], m0).T))
              for p in P]
        yield
        ab = [jnp.where(strict, sc[p][:C, :LANES], 0.0) for p in P]
        akrk = [b16(cat0(jnp.where(strict, sc[p][:C, LANES:], 0.0), jnp.where(incl, sc[p][C:, LANES:], 0.0)))
                for p in P]
        rbb = [b16(jnp.where(incl, sc[p][C:, :LANES], 0.0)) for p in P]

        qb = [b16(ab[p]) for p in P]
        pm = [eye + ab[p] for p in P]
        q = [_dot(qb[p], _bd(qb[p], m0)) for p in P]
        yield
        for _ in range(4):
            qb = [b16(q[p]) for p in P]
            pq = [_dot(cat0(b16(pm[p]), qb[p]), _bd(qb[p], m0)) for p in P]
            pm = [pm[p] + pq[p][:C] for p in P]
            q = [pq[p][C:] for p in P]
            yield
        pm = [b16(pm[p] + _dot(pm[p], _bd(b16(q[p]), m0))) for p in P]
        yield
        lv = [_dot(akrk[p], _bd(d[p]["vb"], m0)) for p in P]
        yield
        sbt = [_bd(b16(state[p]), m0).T for p in P]
        ars = [_dot(cat0(d[p]["atb"], d[p]["rtb"]), sbt[p]) for p in P]
        yield
        ub = [b16(_dot(pm[p], _bd(b16(ars[p][:C] + lv[p][:C]), m0))) for p in P]
        yield
        ys = [ars[p][C:] + lv[p][C:] + _dot(rbb[p], _bd(ub[p], m0)) for p in P]
        yield
        sfull = [_dot(cat0(ub[p], d[p]["vb"]).T, cat0(d[p]["bpb"], d[p]["kpb"])) for p in P]
        for p in P:
            state[p] = state[p] * d[p]["wc"] + jnp.where(m0, sfull[p][:HEAD], sfull[p][HEAD:])
        yield
        mean = head_sum(ys, avg_bd)
        dd = [ys[p] - mean[p] for p in P]
        yield
        var = head_sum([dd[p] * dd[p] for p in P], avg_bd)
        yield
        for p in P:
            ps = slice(p * LANES, (p + 1) * LANES)
            yn = dd[p] * lax.rsqrt(var[p] + GN_EPS) * par_ref[5:6, ps] + par_ref[6:7, ps] + d[p]["bonus"]
            y_ref[0, c * C:(c + 1) * C, ps] = yn.astype(y_ref.dtype)

    def run(gen):
        for _ in gen:
            pass

    def prep_all(c, out):
        half = npairs // 2
        yield from prep(c, range(half), out)
        yield from prep(c, range(half, npairs), out)

    state = [s_sc[p] for p in P]
    d = [None] * npairs
    run(prep_all(0, d))
    for c in range(nch):
        nxt = [None] * npairs
        side = prep_all(c + 1, nxt) if c + 1 < nch else iter(())
        for _ in chunk(c, d, state):
            next(side, None)
        run(side)
        d = nxt
    for p in P:
        s_sc[p] = state[p]
        sout_ref[0, p] = state[p]


def _wkv(hp, hl, par, w2a, s0, *, batch, seq, tb):
    d_inner = hp.shape[1] // 4
    npairs = d_inner // LANES
    hp3 = hp.reshape(batch, seq, 4 * d_inner)
    hl3 = hl.reshape(batch, seq, LANES)
    kern = functools.partial(_wkv_kernel, npairs=npairs)
    return pl.pallas_call(
        kern,
        grid=(batch, seq // tb),
        in_specs=[
            pl.BlockSpec((1, tb, d_inner), lambda b, t: (b, t, 0)),
            pl.BlockSpec((1, tb, d_inner), lambda b, t: (b, t, 1)),
            pl.BlockSpec((1, tb, d_inner), lambda b, t: (b, t, 2)),
            pl.BlockSpec((1, tb, LANES), lambda b, t: (b, t, 0)),
            pl.BlockSpec((8, d_inner), lambda b, t: (0, 0)),
            pl.BlockSpec((LANES, 2 * d_inner), lambda b, t: (0, 0)),
            pl.BlockSpec((npairs, HEAD, LANES), lambda b, t: (0, 0, 0)),
        ],
        out_specs=[
            pl.BlockSpec((1, tb, d_inner), lambda b, t: (b, t, 0)),
            pl.BlockSpec((1, npairs, HEAD, LANES), lambda b, t: (b, 0, 0, 0)),
        ],
        out_shape=[
            jax.ShapeDtypeStruct((batch, seq, d_inner), BF16),
            jax.ShapeDtypeStruct((batch, npairs, HEAD, LANES), F32),
        ],
        scratch_shapes=[pltpu.VMEM((npairs, HEAD, LANES), F32)],
        compiler_params=_cparams(("arbitrary", "arbitrary")),
        name="wkv7",
    )(hp3, hp3, hp3, hl3, par, w2a, s0)


def _outproj_kernel(y_ref, g_ref, res_ref, w_ref, gn_ref, out_ref):
    g = g_ref[...].astype(F32)
    z = y_ref[...].astype(F32) * (g * jax.nn.sigmoid(g))
    m = _dot(z, w_ref[...])
    out_ref[...] = res_ref[...] + _rms(m, gn_ref[...])


def _outproj(y2, gsrc, gcol, res2, w, gn, *, tr):
    rows, di = y2.shape
    d = w.shape[1]
    return pl.pallas_call(
        _outproj_kernel,
        grid=(rows // tr,),
        in_specs=[
            pl.BlockSpec((tr, di), lambda i: (i, 0)),
            pl.BlockSpec((tr, di), lambda i: (i, gcol)),
            pl.BlockSpec((tr, d), lambda i: (i, 0)),
            pl.BlockSpec((di, d), lambda i: (0, 0)),
            pl.BlockSpec((1, d), lambda i: (0, 0)),
        ],
        out_specs=pl.BlockSpec((tr, d), lambda i: (i, 0)),
        out_shape=jax.ShapeDtypeStruct((rows, d), F32),
        compiler_params=_cparams(("arbitrary",)),
        name="gate_outproj",
    )(y2, gsrc, res2, w, gn)


def _mlaproj_kernel(h_ref, gpre_ref, win_ref, qn_ref, wq_ref, kvn_ref, wkn_ref, wvt_ref, rc_ref, rs_ref,
                    oqn_ref, oqr_ref, okn_ref, ovt_ref, okr_ref, og_ref, *, scale):
    u = _rms(h_ref[...], gpre_ref[...])
    t = _dot(u, win_ref[...])
    nq = qn_ref.shape[1]
    nkv = kvn_ref.shape[1]
    di = og_ref.shape[1]
    c_q = t[:, :nq]
    c_kv = t[:, nq:nq + nkv]
    krr = t[:, nq + nkv:nq + nkv + LANES]
    og_ref[...] = t[:, nq + nkv + LANES:].astype(og_ref.dtype)

    rc = rc_ref[...]
    rs = rs_ref[...]
    lane = lax.broadcasted_iota(jnp.int32, krr.shape, 1)
    prod = krr * jnp.where(lane < QK_ROPE, rc, rs)
    okr_ref[...] = (prod + pltpu.roll(prod, QK_ROPE, 1)).astype(okr_ref.dtype)

    q = _dot(_rms(c_q, qn_ref[...]), wq_ref[...]) * scale
    oqn_ref[...] = q[:, :di].astype(oqn_ref.dtype)
    nrep = (di // 2) // LANES
    rct = jnp.concatenate([rc] * nrep, axis=1)
    rst = jnp.concatenate([rs] * nrep, axis=1)
    oqr_ref[...] = (q[:, di:di + di // 2] * rct + q[:, di + di // 2:] * rst).astype(oqr_ref.dtype)

    ckv = _rms(c_kv, kvn_ref[...])
    okn_ref[...] = _dot(ckv, wkn_ref[...]).astype(okn_ref.dtype)
    ovt_ref[...] = _dot_nt(wvt_ref[...], ckv).astype(ovt_ref.dtype)


def _mlaproj(h2, gpre, win, qn, wq, kvn, wkn, wvt, rc, rs, *, tr, scale):
    rows, d = h2.shape
    di = wkn.shape[1]
    full = lambda a: pl.BlockSpec(a.shape, lambda i: (0, 0))
    rowblk = lambda n: pl.BlockSpec((tr, n), lambda i: (i, 0))
    kern = functools.partial(_mlaproj_kernel, scale=scale)
    return pl.pallas_call(
        kern,
        grid=(rows // tr,),
        in_specs=[rowblk(d), full(gpre), full(win), full(qn), full(wq), full(kvn), full(wkn), full(wvt),
                  rowblk(LANES), rowblk(LANES)],
        out_specs=[rowblk(di), rowblk(di // 2), rowblk(di), pl.BlockSpec((di, tr), lambda i: (0, i)),
                   rowblk(LANES), rowblk(di)],
        out_shape=[
            jax.ShapeDtypeStruct((rows, di), BF16),
            jax.ShapeDtypeStruct((rows, di // 2), BF16),
            jax.ShapeDtypeStruct((rows, di), BF16),
            jax.ShapeDtypeStruct((di, rows), BF16),
            jax.ShapeDtypeStruct((rows, LANES), BF16),
            jax.ShapeDtypeStruct((rows, di), BF16),
        ],
        compiler_params=_cparams(("arbitrary",)),
        name="mla_proj",
    )(h2, gpre, win, qn, wq, kvn, wkn, wvt, rc, rs)


NEG = -1e30


def _attn_kernel(qn_ref, qr_ref, kn_ref, kr_ref, vt_ref, knm_ref, krm_ref, vtm_ref, o_ref, s_sc, *, tq, nmeta):
    qi = pl.program_id(2)
    mp = knm_ref.shape[1]
    nh = qn_ref.shape[2] // LANES
    H = range(nh)
    lane = lax.broadcasted_iota(jnp.int32, (tq, LANES), 1)
    hsl = [slice(h * LANES, (h + 1) * LANES) for h in H]

    def query(h):
        qr = qr_ref[0, :, hsl[h // 2]]
        return jnp.concatenate([qn_ref[0, :, hsl[h]],
                                jnp.where((lane // QK_ROPE) == h % 2, qr, jnp.zeros_like(qr))], axis=1)

    qs = [query(h) for h in H]

    def scores(j, h):
        rows = pl.ds(pl.multiple_of(j * tq, tq), tq)
        return _dot_nt(jnp.concatenate([kn_ref[0, rows, hsl[h]], kr_ref[0, rows, :]], axis=1), qs[h])

    def update(carry, s, smax, vt):
        m, acc = carry
        mn = jnp.maximum(m, smax)
        p = jnp.exp2(s - mn)
        vta = jnp.concatenate([vt, jnp.ones((16, vt.shape[1]), BF16)], axis=0)
        return [mn, jnp.exp2(m - mn) * acc + _dot(vta, p)]

    colmax = lambda s: jnp.max(s, axis=0, keepdims=True)

    bmax = []
    for h in H:
        s0 = scores(0, h)
        s_sc[h] = s0
        bmax.append(colmax(s0))

    def step(j, carry):
        nxt = [scores(j + 1, h) for h in H]
        nmax = [colmax(n) for n in nxt]
        rows = pl.ds(pl.multiple_of(j * tq, tq), tq)
        out = []
        for h in H:
            out += update(carry[3 * h:3 * h + 2], s_sc[h], carry[3 * h + 2], vt_ref[0, hsl[h], rows])
            out.append(nmax[h])
        for h in H:
            s_sc[h] = nxt[h]
        return out

    row1 = lambda val: jnp.full((1, tq), val, F32)
    carry = []
    for h in H:
        carry += [row1(NEG), jnp.zeros((LANES + 16, tq), F32), bmax[h]]
    carry = lax.fori_loop(0, qi, step, carry)

    krow = lax.broadcasted_iota(jnp.int32, (mp + tq, tq), 0)
    qcol = lax.broadcasted_iota(jnp.int32, (mp + tq, tq), 1)
    valid = (krow < nmeta) | ((krow >= mp) & ((krow - mp) <= qcol))
    drows = pl.ds(pl.multiple_of(qi * tq, tq), tq)
    sm = [_dot_nt(jnp.concatenate([knm_ref[0, :, hsl[h]], krm_ref[0]], axis=1), qs[h]) for h in H]
    for h in H:
        s = jnp.where(valid, jnp.concatenate([sm[h], s_sc[h]], axis=0), NEG)
        vt = jnp.concatenate([vtm_ref[0, hsl[h], :], vt_ref[0, hsl[h], drows]], axis=1)
        m, acc = update(carry[3 * h:3 * h + 2], s, colmax(s), vt)
        o_ref[0, :, hsl[h]] = (acc[:LANES] / acc[LANES:LANES + 1]).T.astype(o_ref.dtype)


def _attention(qn, qr, kn, kr, vt, knm, krm, vtm, *, batch, seq, tq, nmeta, nh):
    di = qn.shape[1]
    hw = nh * LANES
    r3 = lambda a: a.reshape(batch, seq, a.shape[1])
    mp = knm.shape[0]
    kern = functools.partial(_attn_kernel, tq=tq, nmeta=nmeta)
    return pl.pallas_call(
        kern,
        grid=(batch, di // hw, seq // tq),
        in_specs=[
            pl.BlockSpec((1, tq, hw), lambda b, p, i: (b, i, p)),
            pl.BlockSpec((1, tq, hw // 2), lambda b, p, i: (b, i, p)),
            pl.BlockSpec((1, seq, hw), lambda b, p, i: (b, 0, p)),
            pl.BlockSpec((1, seq, LANES), lambda b, p, i: (b, 0, 0)),
            pl.BlockSpec((1, hw, seq), lambda b, p, i: (0, p, b)),
            pl.BlockSpec((1, mp, hw), lambda b, p, i: (0, 0, p)),
            pl.BlockSpec((1, mp, LANES), lambda b, p, i: (0, 0, 0)),
            pl.BlockSpec((1, hw, mp), lambda b, p, i: (0, p, 0)),
        ],
        out_specs=pl.BlockSpec((1, tq, hw), lambda b, p, i: (b, i, p)),
        out_shape=jax.ShapeDtypeStruct((batch, seq, di), BF16),
        scratch_shapes=[pltpu.VMEM((nh, tq, tq), F32)],
        compiler_params=_cparams(("arbitrary", "arbitrary", "arbitrary")),
        name="mla_attention",
    )(r3(qn), r3(qr), r3(kn), r3(kr), vt[None], knm[None], krm[None], vtm[None])


def _pick(n, cands):
    for c in cands:
        if n % c == 0:
            return c
    return n


def kernel(x, meta_tokens, norm_pre, norm_post, rwkv_mu, rwkv_w_in, rwkv_w0, rwkv_w2, rwkv_a0, rwkv_a2,
           rwkv_k_k, rwkv_k_a, rwkv_r_k, rwkv_ln_w, rwkv_ln_b, rwkv_w_out,
           mla_w_in, mla_q_norm, mla_w_q_up, mla_kv_norm, mla_w_kv_up, mla_w_out):
    B, T, D = x.shape
    DI = rwkv_w_out.shape[1]
    NP = DI // LANES
    MH = DI // V_HEAD
    nq = mla_q_norm.shape[1]
    nkv = mla_kv_norm.shape[1]
    MP = CHUNK
    assert meta_tokens.shape[0] == N_META and T % CHUNK == 0

    w_in = rwkv_w_in[0]
    w_main = w_in[:, :4 * DI].astype(BF16)
    w_lora = w_in[:, 4 * DI:].astype(BF16)
    mu_main = rwkv_mu[0, :4].reshape(4, 1, D)
    mu_lora = rwkv_mu[0, 4:].reshape(2, 1, D)
    zeros1 = jnp.zeros((DI,), F32)
    par = jnp.stack([rwkv_w0[0], rwkv_a0[0], rwkv_k_k[0], rwkv_k_a[0], rwkv_r_k[0],
                     rwkv_ln_w[0], rwkv_ln_b[0], zeros1])
    zblk = jnp.zeros_like(rwkv_w2[0])
    w2a = jnp.concatenate([jnp.concatenate([rwkv_w2[0], zblk], axis=1),
                           jnp.concatenate([zblk, rwkv_a2[0]], axis=1)], axis=0).astype(BF16)
    w_out0 = rwkv_w_out[0].astype(BF16)

    mw = mla_w_in[0]
    kr_w = mw[:, nq + nkv:nq + nkv + QK_ROPE]
    half = QK_ROPE // 2
    rot = lambda wr: jnp.concatenate([-wr[..., half:], wr[..., :half]], axis=-1)
    win2 = jnp.concatenate([mw[:, :nq + nkv], kr_w, rot(kr_w), mw[:, nq + nkv + QK_ROPE:]],
                           axis=1).astype(BF16)
    wq = mla_w_q_up[0].reshape(nq, MH, QK_NOPE + QK_ROPE)
    wq_r = wq[:, :, QK_NOPE:]
    wq2 = jnp.concatenate([wq[:, :, :QK_NOPE].reshape(nq, MH * QK_NOPE),
                           wq_r.reshape(nq, MH * QK_ROPE),
                           rot(wq_r).reshape(nq, MH * QK_ROPE)], axis=1).astype(BF16)
    wkv = mla_w_kv_up[0].reshape(nkv, MH, QK_NOPE + V_HEAD)
    wkn = wkv[:, :, :QK_NOPE].reshape(nkv, MH * QK_NOPE).astype(BF16)
    wvt = wkv[:, :, QK_NOPE:].reshape(nkv, MH * V_HEAD).T.astype(BF16)
    w_out1 = mla_w_out[0].astype(BF16)

    pos = jnp.arange(N_META + T, dtype=F32)
    inv_freq = jnp.exp(-math.log(ROPE_THETA) * jnp.arange(half, dtype=F32) / half)
    ang = pos[:, None] * inv_freq[None, :]
    rc = jnp.tile(jnp.cos(ang), (1, 4))
    rs = jnp.tile(jnp.sin(ang), (1, 4))
    scale = (QK_NOPE + QK_ROPE) ** -0.5 * math.log2(math.e)

    gpre0 = norm_pre[0:1]
    gpre1 = norm_pre[1:2]
    gpost0 = norm_post[0:1]
    gpost1 = norm_post[1:2]

    xm = jnp.concatenate([jnp.zeros((MP - N_META, D), x.dtype), meta_tokens.astype(x.dtype)], axis=0)
    zero8 = jnp.zeros((8, D), x.dtype)
    hp_m, hl_m = _inproj(xm, zero8, gpre0, mu_main, mu_lora, w_main, w_lora, seq=MP, tr=MP)
    s_zero = jnp.zeros((NP, HEAD, LANES), F32)
    y_m, s_meta = _wkv(hp_m, hl_m, par, w2a, s_zero, batch=1, seq=MP, tb=MP)
    h1_m = _outproj(y_m.reshape(MP, DI), hp_m, 3, xm, w_out0, gpost0, tr=MP)
    rc_m = jnp.concatenate([jnp.zeros((MP - N_META, LANES), F32), rc[:N_META]], axis=0)
    rs_m = jnp.concatenate([jnp.zeros((MP - N_META, LANES), F32), rs[:N_META]], axis=0)
    _, _, kn_m, vt_m, kr_m, _ = _mlaproj(h1_m, gpre1, win2, mla_q_norm, wq2, mla_kv_norm, wkn, wvt,
                                         rc_m, rs_m, tr=MP, scale=scale)
    padrows = lambda a: jnp.concatenate(
        [a[MP - N_META:], jnp.zeros((LANES - N_META, a.shape[1]), a.dtype)], axis=0)
    knm, krm = padrows(kn_m), padrows(kr_m)
    vtm = jnp.concatenate([vt_m[:, MP - N_META:], jnp.zeros((DI, LANES - N_META), vt_m.dtype)], axis=1)

    x2 = x.reshape(B * T, D)
    tr_in = _pick(T, (1024, 512, 256, 128, 64))
    hp, hl = _inproj(x2, meta_tokens[N_META - 8:].astype(x.dtype), gpre0, mu_main, mu_lora,
                     w_main, w_lora, seq=T, tr=tr_in)
    y, _ = _wkv(hp, hl, par, w2a, s_meta[0], batch=B, seq=T, tb=_pick(T, (4 * CHUNK, 2 * CHUNK, CHUNK)))
    tr_o = _pick(T, (512, 256, 128, 64))
    h1 = _outproj(y.reshape(B * T, DI), hp, 3, x2, w_out0, gpost0, tr=tr_o)

    tr_p = _pick(T, (256, 128, 64))
    rc_t = jnp.tile(rc[N_META:], (B, 1))
    rs_t = jnp.tile(rs[N_META:], (B, 1))
    qn, qr, kn, vt, kr, g1 = _mlaproj(h1, gpre1, win2, mla_q_norm, wq2, mla_kv_norm, wkn, wvt,
                                      rc_t, rs_t, tr=tr_p, scale=scale)
    tq = _pick(T, (512, 256, 128))
    o = _attention(qn, qr, kn, kr, vt, knm, krm, vtm, batch=B, seq=T, tq=tq, nmeta=N_META, nh=4)
    out = _outproj(o.reshape(B * T, DI), g1, 0, h1, w_out1, gpost1, tr=tr_o)
    return out.reshape(B, T, D)
```

```python
import functools
import math

import jax
import jax.numpy as jnp
from jax import lax
from jax.experimental import pallas as pl
from jax.experimental.pallas import tpu as pltpu

F32 = jnp.float32
BF16 = jnp.bfloat16

RMS_EPS = 1e-6
GN_EPS = 64e-5
ROPE_THETA = 10000.0
N_META = 16
HEAD = 64
LANES = 128
CHUNK = 64
QK_NOPE = 128
QK_ROPE = 64
V_HEAD = 128
VMEM_LIMIT = 56 * 1024 * 1024


def _cparams(sem):
    return pltpu.CompilerParams(dimension_semantics=sem, vmem_limit_bytes=VMEM_LIMIT)


def _rms(x, g):
    return x * lax.rsqrt(jnp.mean(x * x, axis=-1, keepdims=True) + RMS_EPS) * g


def _dot(a, b):
    return jnp.dot(a.astype(BF16), b.astype(BF16), preferred_element_type=F32)


def _dot_nt(a, b):
    return lax.dot_general(a.astype(BF16), b.astype(BF16), (((1,), (1,)), ((), ())),
                           preferred_element_type=F32)


def _dot_tn(a, b):
    return lax.dot_general(a.astype(BF16), b.astype(BF16), (((0,), (0,)), ((), ())),
                           preferred_element_type=F32)


def _inproj_kernel(x_ref, xprev_ref, x0_ref, g_ref, mu_ref, mul_ref, w_ref, wl_ref,
                   out_ref, outl_ref, u_sc, dx_sc, *, tiles_per_seq):
    i = pl.program_id(0)
    j = pl.program_id(1)

    @pl.when(j == 0)
    def _():
        g = g_ref[...]
        u = _rms(x_ref[...], g)
        first = (i % tiles_per_seq) == 0
        prow = jnp.where(first, x0_ref[7:8, :], xprev_ref[7:8, :])
        up = _rms(prow, g)
        rows = lax.broadcasted_iota(jnp.int32, u.shape, 0)
        ush = jnp.where(rows == 0, up, pltpu.roll(u, 1, 0))
        dx = ush - u
        u_sc[...] = u
        dx_sc[...] = dx
        wl = wl_ref[...]
        ow = _dot(u + mul_ref[0] * dx, wl)
        oa = _dot(u + mul_ref[1] * dx, wl)
        lane = lax.broadcasted_iota(jnp.int32, ow.shape, 1)
        outl_ref[...] = jnp.where(lane < HEAD, ow, oa).astype(outl_ref.dtype)

    xg = u_sc[...] + mu_ref[0] * dx_sc[...]
    out_ref[...] = _dot(xg, w_ref[...]).astype(out_ref.dtype)


def _inproj(x2, x0, g, mu_main, mu_lora, w_main, w_lora, *, seq, tr):
    rows, d = x2.shape
    ncol = w_main.shape[1]
    ngroups = mu_main.shape[0]
    gw = ncol // ngroups
    tpb = tr // 8
    kern = functools.partial(_inproj_kernel, tiles_per_seq=seq // tr)
    return pl.pallas_call(
        kern,
        grid=(rows // tr, ngroups),
        in_specs=[
            pl.BlockSpec((tr, d), lambda i, j: (i, 0)),
            pl.BlockSpec((8, d), lambda i, j: (jnp.maximum(i * tpb - 1, 0), 0)),
            pl.BlockSpec((8, d), lambda i, j: (0, 0)),
            pl.BlockSpec((1, d), lambda i, j: (0, 0)),
            pl.BlockSpec((1, 1, d), lambda i, j: (j, 0, 0)),
            pl.BlockSpec((2, 1, d), lambda i, j: (0, 0, 0)),
            pl.BlockSpec((d, gw), lambda i, j: (0, j)),
            pl.BlockSpec((d, LANES), lambda i, j: (0, 0)),
        ],
        out_specs=[
            pl.BlockSpec((tr, gw), lambda i, j: (i, j)),
            pl.BlockSpec((tr, LANES), lambda i, j: (i, 0)),
        ],
        out_shape=[
            jax.ShapeDtypeStruct((rows, ncol), BF16),
            jax.ShapeDtypeStruct((rows, LANES), BF16),
        ],
        scratch_shapes=[pltpu.VMEM((tr, d), F32), pltpu.VMEM((tr, d), F32)],
        compiler_params=_cparams(("arbitrary", "arbitrary")),
        name="rwkv_inproj",
    )(x2, x2, x0, g, mu_main, mu_lora, w_main, w_lora)


def _bd(x, m0):
    zero = jnp.zeros_like(x)
    return jnp.concatenate([jnp.where(m0, x, zero), jnp.where(m0, zero, x)], axis=0)


def _wkv_kernel(r_ref, k_ref, v_ref, lo_ref, par_ref, w2a_ref, s0_ref, y_ref, sout_ref, s_sc, *, npairs):
    C = CHUNK
    di = npairs * LANES
    nch = r_ref.shape[1] // C

    @pl.when(pl.program_id(1) == 0)
    def _():
        s_sc[...] = s0_ref[...]

    row2 = lax.broadcasted_iota(jnp.int32, (LANES, LANES), 0)
    lane2 = lax.broadcasted_iota(jnp.int32, (LANES, LANES), 1)
    samehead = (row2 // HEAD) == (lane2 // HEAD)
    ones_bd = jnp.where(samehead, 1.0, 0.0).astype(BF16)
    avg_bd = jnp.where(samehead, 1.0 / HEAD, 0.0).astype(BF16)
    P = range(npairs)

    tri =jnp.where(lax.broadcasted_iota(jnp.int32, (C, C), 0)
                    >= lax.broadcasted_iota(jnp.int32, (C, C), 1), 1.0, 0.0).astype(BF16)
    tpos = lax.broadcasted_iota(jnp.int32, (C, LANES), 0)
    lane_c = lax.broadcasted_iota(jnp.int32, (C, LANES), 1)
    spos = lane_c % HEAD
    m0 = lane_c < HEAD
    strict = tpos > spos
    incl = tpos >= spos
    eye = jnp.where(tpos == spos, 1.0, 0.0).astype(F32)
    cat0 = lambda x, y: jnp.concatenate([x, y], axis=0)
    cat1 = lambda x, y: jnp.concatenate([x, y], axis=1)
    b16 = lambda x: x.astype(BF16)

    def head_sum(xs, wmat):
        st = _dot(jnp.concatenate(xs, axis=0), wmat)
        return [st[i * C:(i + 1) * C] for i in range(len(xs))]

    def prep(c, pairs, out):
        rows = slice(c * C, (c + 1) * C)
        psl = {p: slice(p * LANES, (p + 1) * LANES) for p in pairs}
        par = lambda n, p: par_ref[n:n + 1, psl[p]]
        lo = lo_ref[0, rows, :].astype(F32)
        tl = b16(jnp.where(lane_c < HEAD, jnp.tanh(lo), lo))
        wa = {p: _dot(tl, cat1(w2a_ref[:, psl[p]], w2a_ref[:, di + p * LANES:di + (p + 1) * LANES])) for p in pairs}
        k = {p: k_ref[0, rows, psl[p]].astype(F32) for p in pairs}
        kk = {p: k[p] * par(2, p) for p in pairs}
        n2 = dict(zip(pairs, head_sum([kk[p] * kk[p] for p in pairs], ones_bd)))
        yield
        lw = {p: -math.exp(-0.5) * jax.nn.sigmoid(par(0, p) + wa[p][:, :LANES]) for p in pairs}
        a = {p: jax.nn.sigmoid(par(1, p) + wa[p][:, LANES:]) for p in pairs}
        hi = {p: b16(lw[p]) for p in pairs}
        cs2 = {p: jnp.dot(tri, cat1(hi[p], b16(lw[p] - hi[p].astype(F32))), preferred_element_type=F32)
               for p in pairs}
        yield
        r = {p: r_ref[0, rows, psl[p]].astype(F32) for p in pairs}
        v = {p: v_ref[0, rows, psl[p]].astype(F32) for p in pairs}
        km = {p: k[p] * (1.0 + (a[p] - 1.0) * par(3, p)) for p in pairs}
        bsum = dict(zip(pairs, head_sum([r[p] * km[p] * par(4, p) for p in pairs], ones_bd)))
        yield
        for p in pairs:
            kkn = kk[p] * lax.rsqrt(jnp.maximum(n2[p], 1e-24))
            bv = kkn * a[p]
            cs = cs2[p][:, :LANES] + cs2[p][:, LANES:]
            last = cs[C - 1:C]
            e_out = jnp.exp(-cs)
            e_l = jnp.exp(last - cs)
            rt = r[p] * jnp.exp(cs)
            out[p] = dict(
                wc=jnp.exp(last), bonus=bsum[p] * v[p],
                vb=b16(v[p]), rtb=b16(rt), atb=b16(-kkn * jnp.exp(cs - lw[p])), ktb=b16(km[p] * e_out),
                btb=b16(bv * e_out), kpb=b16(km[p] * e_l), bpb=b16(bv * e_l))
        yield

    def chunk(c, d, state):
        sc = [_dot(cat0(d[p]["atb"], d[p]["rtb"]), cat1(_bd(d[p]["btb"], m0).T, _bd(d[p]["ktb"], m0).T))
              for p in P]
        yield
        ab = [jnp.where(strict, sc[p][:C, :LANES], 0.0) for p in P]
        akrk = [b16(cat0(jnp.where(strict, sc[p][:C, LANES:], 0.0), jnp.where(incl, sc[p][C:, LANES:], 0.0)))
                for p in P]
        rbb = [b16(jnp.where(incl, sc[p][C:, :LANES], 0.0)) for p in P]

        qb = [b16(ab[p]) for p in P]
        pm = [eye + ab[p] for p in P]
        q = [_dot(qb[p], _bd(qb[p], m0)) for p in P]
        yield
        for _ in range(4):
            qb = [b16(q[p]) for p in P]
            pq = [_dot(cat0(b16(pm[p]), qb[p]), _bd(qb[p], m0)) for p in P]
            pm = [pm[p] + pq[p][:C] for p in P]
            q = [pq[p][C:] for p in P]
            yield
        pm = [b16(pm[p] + _dot(pm[p], _bd(b16(q[p]), m0))) for p in P]
        yield
        lv = [_dot(akrk[p], _bd(d[p]["vb"], m0)) for p in P]
        yield
        sbt = [_bd(b16(state[p]), m0).T for p in P]
        ars = [_dot(cat0(d[p]["atb"], d[p]["rtb"]), sbt[p]) for p in P]
        yield
        ub = [b16(_dot(pm[p], _bd(b16(ars[p][:C] + lv[p][:C]), m0))) for p in P]
        yield
        ys = [ars[p][C:] + lv[p][C:] + _dot(rbb[p], _bd(ub[p], m0)) for p in P]
        yield
        sfull = [_dot(cat0(ub[p], d[p]["vb"]).T, cat0(d[p]["bpb"], d[p]["kpb"])) for p in P]
        for p in P:
            state[p] = state[p] * d[p]["wc"] + jnp.where(m0, sfull[p][:HEAD], sfull[p][HEAD:])
        yield
        mean = head_sum(ys, avg_bd)
        dd = [ys[p] - mean[p] for p in P]
        yield
        var = head_sum([dd[p] * dd[p] for p in P], avg_bd)
        yield
        for p in P:
            ps = slice(p * LANES, (p + 1) * LANES)
            yn = dd[p] * lax.rsqrt(var[p] + GN_EPS) * par_ref[5:6, ps] + par_ref[6:7, ps] + d[p]["bonus"]
            y_ref[0, c * C:(c + 1) * C, ps] = yn.astype(y_ref.dtype)

    def run(gen):
        for _ in gen:
            pass

    def prep_all(c, out):
        half = npairs // 2
        yield from prep(c, range(half), out)
        yield from prep(c, range(half, npairs), out)

    state = [s_sc[p] for p in P]
    d = [None] * npairs
    run(prep_all(0, d))
    for c in range(nch):
        nxt = [None] * npairs
        side = prep_all(c + 1, nxt) if c + 1 < nch else iter(())
        for _ in chunk(c, d, state):
            next(side, None)
        run(side)
        d = nxt
    for p in P:
        s_sc[p] = state[p]
        sout_ref[0, p] = state[p]


def _wkv(hp, hl, par, w2a, s0, *, batch, seq, tb):
    d_inner = hp.shape[1] // 4
    npairs = d_inner // LANES
    hp3 = hp.reshape(batch, seq, 4 * d_inner)
    hl3 = hl.reshape(batch, seq, LANES)
    kern = functools.partial(_wkv_kernel, npairs=npairs)
    return pl.pallas_call(
        kern,
        grid=(batch, seq // tb),
        in_specs=[
            pl.BlockSpec((1, tb, d_inner), lambda b, t: (b, t, 0)),
            pl.BlockSpec((1, tb, d_inner), lambda b, t: (b, t, 1)),
            pl.BlockSpec((1, tb, d_inner), lambda b, t: (b, t, 2)),
            pl.BlockSpec((1, tb, LANES), lambda b, t: (b, t, 0)),
            pl.BlockSpec((8, d_inner), lambda b, t: (0, 0)),
            pl.BlockSpec((LANES, 2 * d_inner), lambda b, t: (0, 0)),
            pl.BlockSpec((npairs, HEAD, LANES), lambda b, t: (0, 0, 0)),
        ],
        out_specs=[
            pl.BlockSpec((1, tb, d_inner), lambda b, t: (b, t, 0)),
            pl.BlockSpec((1, npairs, HEAD, LANES), lambda b, t: (b, 0, 0, 0)),
        ],
        out_shape=[
            jax.ShapeDtypeStruct((batch, seq, d_inner), BF16),
            jax.ShapeDtypeStruct((batch, npairs, HEAD, LANES), F32),
        ],
        scratch_shapes=[pltpu.VMEM((npairs, HEAD, LANES), F32)],
        compiler_params=_cparams(("arbitrary", "arbitrary")),
        name="wkv7",
    )(hp3, hp3, hp3, hl3, par, w2a, s0)


def _outproj_kernel(y_ref, g_ref, res_ref, w_ref, gn_ref, out_ref):
    g = g_ref[...].astype(F32)
    z = y_ref[...].astype(F32) * (g * jax.nn.sigmoid(g))
    m = _dot(z, w_ref[...])
    out_ref[...] = res_ref[...] + _rms(m, gn_ref[...])


def _outproj(y2, gsrc, gcol, res2, w, gn, *, tr):
    rows, di = y2.shape
    d = w.shape[1]
    return pl.pallas_call(
        _outproj_kernel,
        grid=(rows // tr,),
        in_specs=[
            pl.BlockSpec((tr, di), lambda i: (i, 0)),
            pl.BlockSpec((tr, di), lambda i: (i, gcol)),
            pl.BlockSpec((tr, d), lambda i: (i, 0)),
            pl.BlockSpec((di, d), lambda i: (0, 0)),
            pl.BlockSpec((1, d), lambda i: (0, 0)),
        ],
        out_specs=pl.BlockSpec((tr, d), lambda i: (i, 0)),
        out_shape=jax.ShapeDtypeStruct((rows, d), F32),
        compiler_params=_cparams(("arbitrary",)),
        name="gate_outproj",
    )(y2, gsrc, res2, w, gn)


def _mlaproj_kernel(h_ref, gpre_ref, win_ref, qn_ref, wq_ref, kvn_ref, wkn_ref, wvt_ref, rc_ref, rs_ref,
                    oqn_ref, oqr_ref, okn_ref, ovt_ref, okr_ref, og_ref, *, scale):
    u = _rms(h_ref[...], gpre_ref[...])
    t = _dot(u, win_ref[...])
    nq = qn_ref.shape[1]
    nkv = kvn_ref.shape[1]
    di = og_ref.shape[1]
    c_q = t[:, :nq]
    c_kv = t[:, nq:nq + nkv]
    krr = t[:, nq + nkv:nq + nkv + LANES]
    og_ref[...] = t[:, nq + nkv + LANES:].astype(og_ref.dtype)

    rc = rc_ref[...]
    rs = rs_ref[...]
    lane = lax.broadcasted_iota(jnp.int32, krr.shape, 1)
    prod = krr * jnp.where(lane < QK_ROPE, rc, rs)
    okr_ref[...] = (prod + pltpu.roll(prod, QK_ROPE, 1)).astype(okr_ref.dtype)

    q = _dot(_rms(c_q, qn_ref[...]), wq_ref[...]) * scale
    oqn_ref[...] = q[:, :di].astype(oqn_ref.dtype)
    nrep = (di // 2) // LANES
    rct = jnp.concatenate([rc] * nrep, axis=1)
    rst = jnp.concatenate([rs] * nrep, axis=1)
    oqr_ref[...] = (q[:, di:di + di // 2] * rct + q[:, di + di // 2:] * rst).astype(oqr_ref.dtype)

    ckv = _rms(c_kv, kvn_ref[...])
    okn_ref[...] = _dot(ckv, wkn_ref[...]).astype(okn_ref.dtype)
    ovt_ref[...] = _dot_nt(wvt_ref[...], ckv).astype(ovt_ref.dtype)


def _mlaproj(h2, gpre, win, qn, wq, kvn, wkn, wvt, rc, rs, *, tr, scale):
    rows, d = h2.shape
    di = wkn.shape[1]
    full = lambda a: pl.BlockSpec(a.shape, lambda i: (0, 0))
    rowblk = lambda n: pl.BlockSpec((tr, n), lambda i: (i, 0))
    kern = functools.partial(_mlaproj_kernel, scale=scale)
    return pl.pallas_call(
        kern,
        grid=(rows // tr,),
        in_specs=[rowblk(d), full(gpre), full(win), full(qn), full(wq), full(kvn), full(wkn), full(wvt),
                  rowblk(LANES), rowblk(LANES)],
        out_specs=[rowblk(di), rowblk(di // 2), rowblk(di), pl.BlockSpec((di, tr), lambda i: (0, i)),
                   rowblk(LANES), rowblk(di)],
        out_shape=[
            jax.ShapeDtypeStruct((rows, di), BF16),
            jax.ShapeDtypeStruct((rows, di // 2), BF16),
            jax.ShapeDtypeStruct((rows, di), BF16),
            jax.ShapeDtypeStruct((di, rows), BF16),
            jax.ShapeDtypeStruct((rows, LANES), BF16),
            jax.ShapeDtypeStruct((rows, di), BF16),
        ],
        compiler_params=_cparams(("arbitrary",)),
        name="mla_proj",
    )(h2, gpre, win, qn, wq, kvn, wkn, wvt, rc, rs)


NEG = -1e30


def _attn_kernel(qn_ref, qr_ref, kn_ref, kr_ref, vt_ref, knm_ref, krm_ref, vtm_ref, o_ref, s_sc, qt_sc, *,
                 tq, nmeta):
    qi = pl.program_id(2)
    mp = knm_ref.shape[1]
    nh = qn_ref.shape[2] // LANES
    H = range(nh)
    lane = lax.broadcasted_iota(jnp.int32, (tq, LANES), 1)
    hsl = [slice(h * LANES, (h + 1) * LANES) for h in H]

    def query(h):
        qr = qr_ref[0, :, hsl[h // 2]]
        return jnp.concatenate([qn_ref[0, :, hsl[h]],
                                jnp.where((lane // QK_ROPE) == h % 2, qr, jnp.zeros_like(qr))], axis=1)

    for h in H:
        qt_sc[h] = query(h).T

    def scores(j, h):
        rows = pl.ds(pl.multiple_of(j * tq, tq), tq)
        return _dot(jnp.concatenate([kn_ref[0, rows, hsl[h]], kr_ref[0, rows, :]], axis=1), qt_sc[h])

    def update(carry, s, smax, vt):
        m, acc = carry
        mn = jnp.maximum(m, smax)
        p = jnp.exp2(s - mn)
        vta = jnp.concatenate([vt, jnp.ones((16, vt.shape[1]), BF16)], axis=0)
        return [mn, jnp.exp2(m - mn) * acc + _dot(vta, p)]

    colmax = lambda s: jnp.max(s, axis=0, keepdims=True)

    bmax = []
    for h in H:
        s0 = scores(0, h)
        s_sc[h] = s0
        bmax.append(colmax(s0))

    def step(j, carry):
        rows = pl.ds(pl.multiple_of(j * tq, tq), tq)
        out = []
        for h in H:
            nxt = scores(j + 1, h)
            out += update(carry[3 * h:3 * h + 2], s_sc[h], carry[3 * h + 2], vt_ref[0, hsl[h], rows])
            s_sc[h] = nxt
            out.append(colmax(nxt))
        return out

    row1 = lambda val: jnp.full((1, tq), val, F32)
    carry = []
    for h in H:
        carry += [row1(NEG), jnp.zeros((LANES + 16, tq), F32), bmax[h]]
    carry = lax.fori_loop(0, qi, step, carry)

    krow = lax.broadcasted_iota(jnp.int32, (mp + tq, tq), 0)
    qcol = lax.broadcasted_iota(jnp.int32, (mp + tq, tq), 1)
    valid = (krow < nmeta) | ((krow >= mp) & ((krow - mp) <= qcol))
    drows = pl.ds(pl.multiple_of(qi * tq, tq), tq)
    sm = [_dot(jnp.concatenate([knm_ref[0, :, hsl[h]], krm_ref[0]], axis=1), qt_sc[h]) for h in H]
    for h in H:
        s = jnp.where(valid, jnp.concatenate([sm[h], s_sc[h]], axis=0), NEG)
        vt = jnp.concatenate([vtm_ref[0, hsl[h], :], vt_ref[0, hsl[h], drows]], axis=1)
        m, acc = update(carry[3 * h:3 * h + 2], s, colmax(s), vt)
        o_ref[0, :, hsl[h]] = (acc[:LANES] / acc[LANES:LANES + 1]).T.astype(o_ref.dtype)


def _attention(qn, qr, kn, kr, vt, knm, krm, vtm, *, batch, seq, tq, nmeta, nh):
    di = qn.shape[1]
    hw = nh * LANES
    r3 = lambda a: a.reshape(batch, seq, a.shape[1])
    mp = knm.shape[0]
    kern = functools.partial(_attn_kernel, tq=tq, nmeta=nmeta)
    return pl.pallas_call(
        kern,
        grid=(batch, di // hw, seq // tq),
        in_specs=[
            pl.BlockSpec((1, tq, hw), lambda b, p, i: (b, i, p)),
            pl.BlockSpec((1, tq, hw // 2), lambda b, p, i: (b, i, p)),
            pl.BlockSpec((1, seq, hw), lambda b, p, i: (b, 0, p)),
            pl.BlockSpec((1, seq, LANES), lambda b, p, i: (b, 0, 0)),
            pl.BlockSpec((1, hw, seq), lambda b, p, i: (0, p, b)),
            pl.BlockSpec((1, mp, hw), lambda b, p, i: (0, 0, p)),
            pl.BlockSpec((1, mp, LANES), lambda b, p, i: (0, 0, 0)),
            pl.BlockSpec((1, hw, mp), lambda b, p, i: (0, p, 0)),
        ],
        out_specs=pl.BlockSpec((1, tq, hw), lambda b, p, i: (b, i, p)),
        out_shape=jax.ShapeDtypeStruct((batch, seq, di), BF16),
        scratch_shapes=[pltpu.VMEM((nh, tq, tq), F32), pltpu.VMEM((nh, 2 * LANES, tq), BF16)],
        compiler_params=_cparams(("arbitrary", "arbitrary", "arbitrary")),
        name="mla_attention",
    )(r3(qn), r3(qr), r3(kn), r3(kr), vt[None], knm[None], krm[None], vtm[None])


def _pick(n, cands):
    for c in cands:
        if n % c == 0:
            return c
    return n


def kernel(x, meta_tokens, norm_pre, norm_post, rwkv_mu, rwkv_w_in, rwkv_w0, rwkv_w2, rwkv_a0, rwkv_a2,
           rwkv_k_k, rwkv_k_a, rwkv_r_k, rwkv_ln_w, rwkv_ln_b, rwkv_w_out,
           mla_w_in, mla_q_norm, mla_w_q_up, mla_kv_norm, mla_w_kv_up, mla_w_out):
    B, T, D = x.shape
    DI = rwkv_w_out.shape[1]
    NP = DI // LANES
    MH = DI // V_HEAD
    nq = mla_q_norm.shape[1]
    nkv = mla_kv_norm.shape[1]
    MP = CHUNK
    assert meta_tokens.shape[0] == N_META and T % CHUNK == 0

    w_in = rwkv_w_in[0]
    w_main = w_in[:, :4 * DI].astype(BF16)
    w_lora = w_in[:, 4 * DI:].astype(BF16)
    mu_main = rwkv_mu[0, :4].reshape(4, 1, D)
    mu_lora = rwkv_mu[0, 4:].reshape(2, 1, D)
    zeros1 = jnp.zeros((DI,), F32)
    par = jnp.stack([rwkv_w0[0], rwkv_a0[0], rwkv_k_k[0], rwkv_k_a[0], rwkv_r_k[0],
                     rwkv_ln_w[0], rwkv_ln_b[0], zeros1])
    zblk = jnp.zeros_like(rwkv_w2[0])
    w2a = jnp.concatenate([jnp.concatenate([rwkv_w2[0], zblk], axis=1),
                           jnp.concatenate([zblk, rwkv_a2[0]], axis=1)], axis=0).astype(BF16)
    w_out0 = rwkv_w_out[0].astype(BF16)

    mw = mla_w_in[0]
    kr_w = mw[:, nq + nkv:nq + nkv + QK_ROPE]
    half = QK_ROPE // 2
    rot = lambda wr: jnp.concatenate([-wr[..., half:], wr[..., :half]], axis=-1)
    win2 = jnp.concatenate([mw[:, :nq + nkv], kr_w, rot(kr_w), mw[:, nq + nkv + QK_ROPE:]],
                           axis=1).astype(BF16)
    wq = mla_w_q_up[0].reshape(nq, MH, QK_NOPE + QK_ROPE)
    wq_r = wq[:, :, QK_NOPE:]
    wq2 = jnp.concatenate([wq[:, :, :QK_NOPE].reshape(nq, MH * QK_NOPE),
                           wq_r.reshape(nq, MH * QK_ROPE),
                           rot(wq_r).reshape(nq, MH * QK_ROPE)], axis=1).astype(BF16)
    wkv = mla_w_kv_up[0].reshape(nkv, MH, QK_NOPE + V_HEAD)
    wkn = wkv[:, :, :QK_NOPE].reshape(nkv, MH * QK_NOPE).astype(BF16)
    wvt = wkv[:, :, QK_NOPE:].reshape(nkv, MH * V_HEAD).T.astype(BF16)
    w_out1 = mla_w_out[0].astype(BF16)

    pos = jnp.arange(N_META + T, dtype=F32)
    inv_freq = jnp.exp(-math.log(ROPE_THETA) * jnp.arange(half, dtype=F32) / half)
    ang = pos[:, None] * inv_freq[None, :]
    rc = jnp.tile(jnp.cos(ang), (1, 4))
    rs = jnp.tile(jnp.sin(ang), (1, 4))
    scale = (QK_NOPE + QK_ROPE) ** -0.5 * math.log2(math.e)

    gpre0 = norm_pre[0:1]
    gpre1 = norm_pre[1:2]
    gpost0 = norm_post[0:1]
    gpost1 = norm_post[1:2]

    xm = jnp.concatenate([jnp.zeros((MP - N_META, D), x.dtype), meta_tokens.astype(x.dtype)], axis=0)
    zero8 = jnp.zeros((8, D), x.dtype)
    hp_m, hl_m = _inproj(xm, zero8, gpre0, mu_main, mu_lora, w_main, w_lora, seq=MP, tr=MP)
    s_zero = jnp.zeros((NP, HEAD, LANES), F32)
    y_m, s_meta = _wkv(hp_m, hl_m, par, w2a, s_zero, batch=1, seq=MP, tb=MP)
    h1_m = _outproj(y_m.reshape(MP, DI), hp_m, 3, xm, w_out0, gpost0, tr=MP)
    rc_m = jnp.concatenate([jnp.zeros((MP - N_META, LANES), F32), rc[:N_META]], axis=0)
    rs_m = jnp.concatenate([jnp.zeros((MP - N_META, LANES), F32), rs[:N_META]], axis=0)
    _, _, kn_m, vt_m, kr_m, _ = _mlaproj(h1_m, gpre1, win2, mla_q_norm, wq2, mla_kv_norm, wkn, wvt,
                                         rc_m, rs_m, tr=MP, scale=scale)
    padrows = lambda a: jnp.concatenate(
        [a[MP - N_META:], jnp.zeros((LANES - N_META, a.shape[1]), a.dtype)], axis=0)
    knm, krm = padrows(kn_m), padrows(kr_m)
    vtm = jnp.concatenate([vt_m[:, MP - N_META:], jnp.zeros((DI, LANES - N_META), vt_m.dtype)], axis=1)

    x2 = x.reshape(B * T, D)
    tr_in = _pick(T, (1024, 512, 256, 128, 64))
    hp, hl = _inproj(x2, meta_tokens[N_META - 8:].astype(x.dtype), gpre0, mu_main, mu_lora,
                     w_main, w_lora, seq=T, tr=tr_in)
    y, _ = _wkv(hp, hl, par, w2a, s_meta[0], batch=B, seq=T, tb=_pick(T, (4 * CHUNK, 2 * CHUNK, CHUNK)))
    tr_o = _pick(T, (1024, 512, 256, 128, 64))
    h1 = _outproj(y.reshape(B * T, DI), hp, 3, x2, w_out0, gpost0, tr=tr_o)

    tr_p = _pick(T, (256, 128, 64))
    rc_t = jnp.tile(rc[N_META:], (B, 1))
    rs_t = jnp.tile(rs[N_META:], (B, 1))
    qn, qr, kn, vt, kr, g1 = _mlaproj(h1, gpre1, win2, mla_q_norm, wq2, mla_kv_norm, wkn, wvt,
                                      rc_t, rs_t, tr=tr_p, scale=scale)
    tq = _pick(T, (512, 256, 128))
    o = _attention(qn, qr, kn, kr, vt, knm, krm, vtm, batch=B, seq=T, tq=tq, nmeta=N_META, nh=4)
    out = _outproj(o.reshape(B * T, DI), g1, 0, h1, w_out1, gpost1, tr=tr_o)
    return out.reshape(B, T, D)
```

```python
import functools
import math

import jax
import jax.numpy as jnp
from jax import lax
from jax.experimental import pallas as pl
from jax.experimental.pallas import tpu as pltpu

F32 = jnp.float32
BF16 = jnp.bfloat16

RMS_EPS = 1e-6
GN_EPS = 64e-5
ROPE_THETA = 10000.0
N_META = 16
HEAD = 64
LANES = 128
CHUNK = 64
QK_NOPE = 128
QK_ROPE = 64
V_HEAD = 128
VMEM_LIMIT = 56 * 1024 * 1024


def _cparams(sem):
    return pltpu.CompilerParams(dimension_semantics=sem, vmem_limit_bytes=VMEM_LIMIT)


def _rms(x, g):
    return x * lax.rsqrt(jnp.mean(x * x, axis=-1, keepdims=True) + RMS_EPS) * g


def _dot(a, b):
    return jnp.dot(a.astype(BF16), b.astype(BF16), preferred_element_type=F32)


def _dot_nt(a, b):
    return lax.dot_general(a.astype(BF16), b.astype(BF16), (((1,), (1,)), ((), ())),
                           preferred_element_type=F32)


def _dot_tn(a, b):
    return lax.dot_general(a.astype(BF16), b.astype(BF16), (((0,), (0,)), ((), ())),
                           preferred_element_type=F32)


def _inproj_kernel(x_ref, xprev_ref, x0_ref, g_ref, mu_ref, mul_ref, w_ref, wl_ref,
                   out_ref, outl_ref, u_sc, dx_sc, *, tiles_per_seq):
    i = pl.program_id(0)
    j = pl.program_id(1)

    @pl.when(j == 0)
    def _():
        g = g_ref[...]
        u = _rms(x_ref[...], g)
        first = (i % tiles_per_seq) == 0
        prow = jnp.where(first, x0_ref[7:8, :], xprev_ref[7:8, :])
        up = _rms(prow, g)
        rows = lax.broadcasted_iota(jnp.int32, u.shape, 0)
        ush = jnp.where(rows == 0, up, pltpu.roll(u, 1, 0))
        dx = ush - u
        u_sc[...] = u
        dx_sc[...] = dx
        wl = wl_ref[...]
        ow = _dot(u + mul_ref[0] * dx, wl)
        oa = _dot(u + mul_ref[1] * dx, wl)
        lane = lax.broadcasted_iota(jnp.int32, ow.shape, 1)
        outl_ref[...] = jnp.where(lane < HEAD, ow, oa).astype(outl_ref.dtype)

    xg = u_sc[...] + mu_ref[0] * dx_sc[...]
    out_ref[...] = _dot(xg, w_ref[...]).astype(out_ref.dtype)


def _inproj(x2, x0, g, mu_main, mu_lora, w_main, w_lora, *, seq, tr):
    rows, d = x2.shape
    ncol = w_main.shape[1]
    ngroups = mu_main.shape[0]
    gw = ncol // ngroups
    tpb = tr // 8
    kern = functools.partial(_inproj_kernel, tiles_per_seq=seq // tr)
    return pl.pallas_call(
        kern,
        grid=(rows // tr, ngroups),
        in_specs=[
            pl.BlockSpec((tr, d), lambda i, j: (i, 0)),
            pl.BlockSpec((8, d), lambda i, j: (jnp.maximum(i * tpb - 1, 0), 0)),
            pl.BlockSpec((8, d), lambda i, j: (0, 0)),
            pl.BlockSpec((1, d), lambda i, j: (0, 0)),
            pl.BlockSpec((1, 1, d), lambda i, j: (j, 0, 0)),
            pl.BlockSpec((2, 1, d), lambda i, j: (0, 0, 0)),
            pl.BlockSpec((d, gw), lambda i, j: (0, j)),
            pl.BlockSpec((d, LANES), lambda i, j: (0, 0)),
        ],
        out_specs=[
            pl.BlockSpec((tr, gw), lambda i, j: (i, j)),
            pl.BlockSpec((tr, LANES), lambda i, j: (i, 0)),
        ],
        out_shape=[
            jax.ShapeDtypeStruct((rows, ncol), BF16),
            jax.ShapeDtypeStruct((rows, LANES), BF16),
        ],
        scratch_shapes=[pltpu.VMEM((tr, d), F32), pltpu.VMEM((tr, d), F32)],
        compiler_params=_cparams(("arbitrary", "arbitrary")),
        name="rwkv_inproj",
    )(x2, x2, x0, g, mu_main, mu_lora, w_main, w_lora)


def _bd(x, m0):
    zero = jnp.zeros_like(x)
    return jnp.concatenate([jnp.where(m0, x, zero), jnp.where(m0, zero, x)], axis=0)


def _wkv_kernel(r_ref, k_ref, v_ref, lo_ref, par_ref, w2a_ref, s0_ref, y_ref, sout_ref, s_sc, *, npairs):
    C = CHUNK
    di = npairs * LANES
    nch = r_ref.shape[1] // C

    @pl.when(pl.program_id(1) == 0)
    def _():
        s_sc[...] = s0_ref[...]

    row2 = lax.broadcasted_iota(jnp.int32, (LANES, LANES), 0)
    lane2 = lax.broadcasted_iota(jnp.int32, (LANES, LANES), 1)
    samehead = (row2 // HEAD) == (lane2 // HEAD)
    ones_bd = jnp.where(samehead, 1.0, 0.0).astype(BF16)
    avg_bd = jnp.where(samehead, 1.0 / HEAD, 0.0).astype(BF16)
    P = range(npairs)

    tri =jnp.where(lax.broadcasted_iota(jnp.int32, (C, C), 0)
                    >= lax.broadcasted_iota(jnp.int32, (C, C), 1), 1.0, 0.0).astype(BF16)
    tpos = lax.broadcasted_iota(jnp.int32, (C, LANES), 0)
    lane_c = lax.broadcasted_iota(jnp.int32, (C, LANES), 1)
    spos = lane_c % HEAD
    m0 = lane_c < HEAD
    strict = tpos > spos
    incl = tpos >= spos
    eye = jnp.where(tpos == spos, 1.0, 0.0).astype(F32)
    cat0 = lambda x, y: jnp.concatenate([x, y], axis=0)
    cat1 = lambda x, y: jnp.concatenate([x, y], axis=1)
    b16 = lambda x: x.astype(BF16)

    def head_sum(xs, wmat):
        st = _dot(jnp.concatenate(xs, axis=0), wmat)
        return [st[i * C:(i + 1) * C] for i in range(len(xs))]

    lo = lo_ref[0].astype(F32)
    tl = b16(jnp.where(lax.broadcasted_iota(jnp.int32, lo.shape, 1) < HEAD, jnp.tanh(lo), lo))
    wa_all = [_dot(tl, cat1(w2a_ref[:, p * LANES:(p + 1) * LANES], w2a_ref[:, di + p * LANES:di + (p + 1) * LANES]))
              for p in P]

    def prep(c, pairs, out):
        rows = slice(c * C, (c + 1) * C)
        psl = {p: slice(p * LANES, (p + 1) * LANES) for p in pairs}
        par = lambda n, p: par_ref[n:n + 1, psl[p]]
        wa = {p: wa_all[p][rows] for p in pairs}
        k = {p: k_ref[0, rows, psl[p]].astype(F32) for p in pairs}
        kk = {p: k[p] * par(2, p) for p in pairs}
        n2 = dict(zip(pairs, head_sum([kk[p] * kk[p] for p in pairs], ones_bd)))
        yield
        lw = {p: -math.exp(-0.5) * jax.nn.sigmoid(par(0, p) + wa[p][:, :LANES]) for p in pairs}
        a = {p: jax.nn.sigmoid(par(1, p) + wa[p][:, LANES:]) for p in pairs}
        hi = {p: b16(lw[p]) for p in pairs}
        cs2 = {p: jnp.dot(tri, cat1(hi[p], b16(lw[p] - hi[p].astype(F32))), preferred_element_type=F32)
               for p in pairs}
        yield
        r = {p: r_ref[0, rows, psl[p]].astype(F32) for p in pairs}
        v = {p: v_ref[0, rows, psl[p]].astype(F32) for p in pairs}
        km = {p: k[p] * (1.0 + (a[p] - 1.0) * par(3, p)) for p in pairs}
        bsum = dict(zip(pairs, head_sum([r[p] * km[p] * par(4, p) for p in pairs], ones_bd)))
        yield
        for p in pairs:
            kkn = kk[p] * lax.rsqrt(jnp.maximum(n2[p], 1e-24))
            bv = kkn * a[p]
            cs = cs2[p][:, :LANES] + cs2[p][:, LANES:]
            last = cs[C - 1:C]
            e_out = jnp.exp(-cs)
            e_l = jnp.exp(last - cs)
            rt = r[p] * jnp.exp(cs)
            out[p] = dict(
                wc=jnp.exp(last), bonus=bsum[p] * v[p],
                vb=b16(v[p]), rtb=b16(rt), atb=b16(-kkn * jnp.exp(cs - lw[p])),
                kpb=b16(km[p] * e_l), bpb=b16(bv * e_l),
                zt=cat1(_bd(b16(bv * e_out), m0).T, _bd(b16(km[p] * e_out), m0).T))
        yield

    def staged(fn, out):
        for p in P:
            out.append(fn(p))
            if p % 4 == 3:
                yield

    def indep(d, res):
        sc = []
        yield from staged(lambda p: _dot(cat0(d[p]["atb"], d[p]["rtb"]), d[p]["zt"]), sc)
        ab = [jnp.where(strict, sc[p][:C, :LANES], 0.0) for p in P]
        akrk = [b16(cat0(jnp.where(strict, sc[p][:C, LANES:], 0.0), jnp.where(incl, sc[p][C:, LANES:], 0.0)))
                for p in P]
        res["rbb"] = [b16(jnp.where(incl, sc[p][C:, :LANES], 0.0)) for p in P]

        qb = [b16(ab[p]) for p in P]
        pm = [eye + ab[p] for p in P]
        q = []
        yield from staged(lambda p: _dot(qb[p], _bd(qb[p], m0)), q)
        for _ in range(4):
            qb = [b16(q[p]) for p in P]
            pq = []
            yield from staged(lambda p: _dot(cat0(b16(pm[p]), qb[p]), _bd(qb[p], m0)), pq)
            pm = [pm[p] + pq[p][:C] for p in P]
            q = [pq[p][C:] for p in P]
        res["pm"] = []
        yield from staged(lambda p: b16(pm[p] + _dot(pm[p], _bd(b16(q[p]), m0))), res["pm"])
        res["lv"] = []
        yield from staged(lambda p: _dot(akrk[p], _bd(d[p]["vb"], m0)), res["lv"])

    def dep(c, d, res, state):
        pm, lv, rbb = res["pm"], res["lv"], res["rbb"]
        ars = []
        yield from staged(lambda p: _dot(cat0(d[p]["atb"], d[p]["rtb"]), _bd(b16(state[p]), m0).T), ars)
        ub = []
        yield from staged(lambda p: b16(_dot(pm[p], _bd(b16(ars[p][:C] + lv[p][:C]), m0))), ub)
        ys = []
        yield from staged(lambda p: ars[p][C:] + lv[p][C:] + _dot(rbb[p], _bd(ub[p], m0)), ys)
        sfull = []
        yield from staged(lambda p: _dot(cat0(ub[p], d[p]["vb"]).T, cat0(d[p]["bpb"], d[p]["kpb"])), sfull)
        for p in P:
            state[p] = state[p] * d[p]["wc"] + jnp.where(m0, sfull[p][:HEAD], sfull[p][HEAD:])
        mean = head_sum(ys, avg_bd)
        dd = [ys[p] - mean[p] for p in P]
        yield
        var = head_sum([dd[p] * dd[p] for p in P], avg_bd)
        yield
        for p in P:
            ps = slice(p * LANES, (p + 1) * LANES)
            yn = dd[p] * lax.rsqrt(var[p] + GN_EPS) * par_ref[5:6, ps] + par_ref[6:7, ps] + d[p]["bonus"]
            y_ref[0, c * C:(c + 1) * C, ps] = yn.astype(y_ref.dtype)
            if p % 4 == 3:
                yield

    def prep_all(c, out):
        half = npairs // 2
        yield from prep(c, range(half), out)
        yield from prep(c, range(half, npairs), out)

    def interleave(gens):
        gens = list(gens)
        while gens:
            for g in list(gens):
                if next(g, StopIteration) is StopIteration:
                    gens.remove(g)

    state = [s_sc[p] for p in P]
    prepd = [[None] * npairs for _ in range(nch)]
    res = [dict() for _ in range(nch)]
    interleave([prep_all(0, prepd[0])])
    for ph in range(nch + 1):
        gens = []
        if ph >= 1:
            gens.append(dep(ph - 1, prepd[ph - 1], res[ph - 1], state))
        if ph < nch:
            gens.append(indep(prepd[ph], res[ph]))
        if ph + 1 < nch:
            gens.append(prep_all(ph + 1, prepd[ph + 1]))
        interleave(gens)
    for p in P:
        s_sc[p] = state[p]
        sout_ref[0, p] = state[p]


def _wkv(hp, hl, par, w2a, s0, *, batch, seq, tb):
    d_inner = hp.shape[1] // 4
    npairs = d_inner // LANES
    hp3 = hp.reshape(batch, seq, 4 * d_inner)
    hl3 = hl.reshape(batch, seq, LANES)
    kern = functools.partial(_wkv_kernel, npairs=npairs)
    return pl.pallas_call(
        kern,
        grid=(batch, seq // tb),
        in_specs=[
            pl.BlockSpec((1, tb, d_inner), lambda b, t: (b, t, 0)),
            pl.BlockSpec((1, tb, d_inner), lambda b, t: (b, t, 1)),
            pl.BlockSpec((1, tb, d_inner), lambda b, t: (b, t, 2)),
            pl.BlockSpec((1, tb, LANES), lambda b, t: (b, t, 0)),
            pl.BlockSpec((8, d_inner), lambda b, t: (0, 0)),
            pl.BlockSpec((LANES, 2 * d_inner), lambda b, t: (0, 0)),
            pl.BlockSpec((npairs, HEAD, LANES), lambda b, t: (0, 0, 0)),
        ],
        out_specs=[
            pl.BlockSpec((1, tb, d_inner), lambda b, t: (b, t, 0)),
            pl.BlockSpec((1, npairs, HEAD, LANES), lambda b, t: (b, 0, 0, 0)),
        ],
        out_shape=[
            jax.ShapeDtypeStruct((batch, seq, d_inner), BF16),
            jax.ShapeDtypeStruct((batch, npairs, HEAD, LANES), F32),
        ],
        scratch_shapes=[pltpu.VMEM((npairs, HEAD, LANES), F32)],
        compiler_params=_cparams(("arbitrary", "arbitrary")),
        name="wkv7",
    )(hp3, hp3, hp3, hl3, par, w2a, s0)


def _outproj_kernel(y_ref, g_ref, res_ref, w_ref, gn_ref, out_ref):
    g = g_ref[...].astype(F32)
    z = y_ref[...].astype(F32) * (g * jax.nn.sigmoid(g))
    m = _dot(z, w_ref[...])
    out_ref[...] = res_ref[...] + _rms(m, gn_ref[...])


def _outproj(y2, gsrc, gcol, res2, w, gn, *, tr):
    rows, di = y2.shape
    d = w.shape[1]
    return pl.pallas_call(
        _outproj_kernel,
        grid=(rows // tr,),
        in_specs=[
            pl.BlockSpec((tr, di), lambda i: (i, 0)),
            pl.BlockSpec((tr, di), lambda i: (i, gcol)),
            pl.BlockSpec((tr, d), lambda i: (i, 0)),
            pl.BlockSpec((di, d), lambda i: (0, 0)),
            pl.BlockSpec((1, d), lambda i: (0, 0)),
        ],
        out_specs=pl.BlockSpec((tr, d), lambda i: (i, 0)),
        out_shape=jax.ShapeDtypeStruct((rows, d), F32),
        compiler_params=_cparams(("arbitrary",)),
        name="gate_outproj",
    )(y2, gsrc, res2, w, gn)


def _mlaproj_kernel(h_ref, gpre_ref, win_ref, qn_ref, wq_ref, kvn_ref, wkn_ref, wvt_ref, rc_ref, rs_ref,
                    oqn_ref, oqr_ref, okn_ref, ovt_ref, okr_ref, og_ref, *, scale):
    u = _rms(h_ref[...], gpre_ref[...])
    t = _dot(u, win_ref[...])
    nq = qn_ref.shape[1]
    nkv = kvn_ref.shape[1]
    di = og_ref.shape[1]
    c_q = t[:, :nq]
    c_kv = t[:, nq:nq + nkv]
    krr = t[:, nq + nkv:nq + nkv + LANES]
    og_ref[...] = t[:, nq + nkv + LANES:].astype(og_ref.dtype)

    rc = rc_ref[...]
    rs = rs_ref[...]
    lane = lax.broadcasted_iota(jnp.int32, krr.shape, 1)
    prod = krr * jnp.where(lane < QK_ROPE, rc, rs)
    okr_ref[...] = (prod + pltpu.roll(prod, QK_ROPE, 1)).astype(okr_ref.dtype)

    q = _dot(_rms(c_q, qn_ref[...]), wq_ref[...]) * scale
    oqn_ref[...] = q[:, :di].astype(oqn_ref.dtype)
    nrep = (di // 2) // LANES
    rct = jnp.concatenate([rc] * nrep, axis=1)
    rst = jnp.concatenate([rs] * nrep, axis=1)
    oqr_ref[...] = (q[:, di:di + di // 2] * rct + q[:, di + di // 2:] * rst).astype(oqr_ref.dtype)

    ckv = _rms(c_kv, kvn_ref[...])
    okn_ref[...] = _dot(ckv, wkn_ref[...]).astype(okn_ref.dtype)
    ovt_ref[...] = _dot_nt(wvt_ref[...], ckv).astype(ovt_ref.dtype)


def _mlaproj(h2, gpre, win, qn, wq, kvn, wkn, wvt, rc, rs, *, tr, scale):
    rows, d = h2.shape
    di = wkn.shape[1]
    full = lambda a: pl.BlockSpec(a.shape, lambda i: (0, 0))
    rowblk = lambda n: pl.BlockSpec((tr, n), lambda i: (i, 0))
    kern = functools.partial(_mlaproj_kernel, scale=scale)
    return pl.pallas_call(
        kern,
        grid=(rows // tr,),
        in_specs=[rowblk(d), full(gpre), full(win), full(qn), full(wq), full(kvn), full(wkn), full(wvt),
                  rowblk(LANES), rowblk(LANES)],
        out_specs=[rowblk(di), rowblk(di // 2), rowblk(di), pl.BlockSpec((di, tr), lambda i: (0, i)),
                   rowblk(LANES), rowblk(di)],
        out_shape=[
            jax.ShapeDtypeStruct((rows, di), BF16),
            jax.ShapeDtypeStruct((rows, di // 2), BF16),
            jax.ShapeDtypeStruct((rows, di), BF16),
            jax.ShapeDtypeStruct((di, rows), BF16),
            jax.ShapeDtypeStruct((rows, LANES), BF16),
            jax.ShapeDtypeStruct((rows, di), BF16),
        ],
        compiler_params=_cparams(("arbitrary",)),
        name="mla_proj",
    )(h2, gpre, win, qn, wq, kvn, wkn, wvt, rc, rs)


NEG = -1e30


def _attn_kernel(qn_ref, qr_ref, kn_ref, kr_ref, vt_ref, knm_ref, krm_ref, vtm_ref, o_ref, s_sc, qt_sc, *,
                 tq, nmeta):
    qi = pl.program_id(2)
    mp = knm_ref.shape[1]
    nh = qn_ref.shape[2] // LANES
    H = range(nh)
    lane = lax.broadcasted_iota(jnp.int32, (tq, LANES), 1)
    hsl = [slice(h * LANES, (h + 1) * LANES) for h in H]

    def query(h):
        qr = qr_ref[0, :, hsl[h // 2]]
        return jnp.concatenate([qn_ref[0, :, hsl[h]],
                                jnp.where((lane // QK_ROPE) == h % 2, qr, jnp.zeros_like(qr))], axis=1)

    for h in H:
        qt_sc[h] = query(h).T

    def scores(j, h):
        rows = pl.ds(pl.multiple_of(j * tq, tq), tq)
        return _dot(jnp.concatenate([kn_ref[0, rows, hsl[h]], kr_ref[0, rows, :]], axis=1), qt_sc[h])

    def update(carry, s, smax, vt):
        m, acc = carry
        mn = jnp.maximum(m, smax)
        p = jnp.exp2(s - mn)
        vta = jnp.concatenate([vt, jnp.ones((16, vt.shape[1]), BF16)], axis=0)
        return [mn, jnp.exp2(m - mn) * acc + _dot(vta, p)]

    colmax = lambda s: jnp.max(s, axis=0, keepdims=True)

    bmax = []
    for h in H:
        s0 = scores(0, h)
        s_sc[h] = s0
        bmax.append(colmax(s0))

    def step(j, carry):
        rows = pl.ds(pl.multiple_of(j * tq, tq), tq)
        out = []
        for h in H:
            nxt = scores(j + 1, h)
            out += update(carry[3 * h:3 * h + 2], s_sc[h], carry[3 * h + 2], vt_ref[0, hsl[h], rows])
            s_sc[h] = nxt
            out.append(colmax(nxt))
        return out

    row1 = lambda val: jnp.full((1, tq), val, F32)
    carry = []
    for h in H:
        carry += [row1(NEG), jnp.zeros((LANES + 16, tq), F32), bmax[h]]
    carry = lax.fori_loop(0, qi, step, carry)

    krow = lax.broadcasted_iota(jnp.int32, (mp + tq, tq), 0)
    qcol = lax.broadcasted_iota(jnp.int32, (mp + tq, tq), 1)
    valid = (krow < nmeta) | ((krow >= mp) & ((krow - mp) <= qcol))
    drows = pl.ds(pl.multiple_of(qi * tq, tq), tq)
    sm = [_dot(jnp.concatenate([knm_ref[0, :, hsl[h]], krm_ref[0]], axis=1), qt_sc[h]) for h in H]
    for h in H:
        s = jnp.where(valid, jnp.concatenate([sm[h], s_sc[h]], axis=0), NEG)
        vt = jnp.concatenate([vtm_ref[0, hsl[h], :], vt_ref[0, hsl[h], drows]], axis=1)
        m, acc = update(carry[3 * h:3 * h + 2], s, colmax(s), vt)
        o_ref[0, :, hsl[h]] = (acc[:LANES] / acc[LANES:LANES + 1]).T.astype(o_ref.dtype)


def _attention(qn, qr, kn, kr, vt, knm, krm, vtm, *, batch, seq, tq, nmeta, nh):
    di = qn.shape[1]
    hw = nh * LANES
    r3 = lambda a: a.reshape(batch, seq, a.shape[1])
    mp = knm.shape[0]
    kern = functools.partial(_attn_kernel, tq=tq, nmeta=nmeta)
    return pl.pallas_call(
        kern,
        grid=(batch, di // hw, seq // tq),
        in_specs=[
            pl.BlockSpec((1, tq, hw), lambda b, p, i: (b, i, p)),
            pl.BlockSpec((1, tq, hw // 2), lambda b, p, i: (b, i, p)),
            pl.BlockSpec((1, seq, hw), lambda b, p, i: (b, 0, p)),
            pl.BlockSpec((1, seq, LANES), lambda b, p, i: (b, 0, 0)),
            pl.BlockSpec((1, hw, seq), lambda b, p, i: (0, p, b)),
            pl.BlockSpec((1, mp, hw), lambda b, p, i: (0, 0, p)),
            pl.BlockSpec((1, mp, LANES), lambda b, p, i: (0, 0, 0)),
            pl.BlockSpec((1, hw, mp), lambda b, p, i: (0, p, 0)),
        ],
        out_specs=pl.BlockSpec((1, tq, hw), lambda b, p, i: (b, i, p)),
        out_shape=jax.ShapeDtypeStruct((batch, seq, di), BF16),
        scratch_shapes=[pltpu.VMEM((nh, tq, tq), F32), pltpu.VMEM((nh, 2 * LANES, tq), BF16)],
        compiler_params=_cparams(("arbitrary", "arbitrary", "arbitrary")),
        name="mla_attention",
    )(r3(qn), r3(qr), r3(kn), r3(kr), vt[None], knm[None], krm[None], vtm[None])


def _pick(n, cands):
    for c in cands:
        if n % c == 0:
            return c
    return n


def kernel(x, meta_tokens, norm_pre, norm_post, rwkv_mu, rwkv_w_in, rwkv_w0, rwkv_w2, rwkv_a0, rwkv_a2,
           rwkv_k_k, rwkv_k_a, rwkv_r_k, rwkv_ln_w, rwkv_ln_b, rwkv_w_out,
           mla_w_in, mla_q_norm, mla_w_q_up, mla_kv_norm, mla_w_kv_up, mla_w_out):
    B, T, D = x.shape
    DI = rwkv_w_out.shape[1]
    NP = DI // LANES
    MH = DI // V_HEAD
    nq = mla_q_norm.shape[1]
    nkv = mla_kv_norm.shape[1]
    MP = CHUNK
    assert meta_tokens.shape[0] == N_META and T % CHUNK == 0

    w_in = rwkv_w_in[0]
    w_main = w_in[:, :4 * DI].astype(BF16)
    w_lora = w_in[:, 4 * DI:].astype(BF16)
    mu_main = rwkv_mu[0, :4].reshape(4, 1, D)
    mu_lora = rwkv_mu[0, 4:].reshape(2, 1, D)
    zeros1 = jnp.zeros((DI,), F32)
    par = jnp.stack([rwkv_w0[0], rwkv_a0[0], rwkv_k_k[0], rwkv_k_a[0], rwkv_r_k[0],
                     rwkv_ln_w[0], rwkv_ln_b[0], zeros1])
    zblk = jnp.zeros_like(rwkv_w2[0])
    w2a = jnp.concatenate([jnp.concatenate([rwkv_w2[0], zblk], axis=1),
                           jnp.concatenate([zblk, rwkv_a2[0]], axis=1)], axis=0).astype(BF16)
    w_out0 = rwkv_w_out[0].astype(BF16)

    mw = mla_w_in[0]
    kr_w = mw[:, nq + nkv:nq + nkv + QK_ROPE]
    half = QK_ROPE // 2
    rot = lambda wr: jnp.concatenate([-wr[..., half:], wr[..., :half]], axis=-1)
    win2 = jnp.concatenate([mw[:, :nq + nkv], kr_w, rot(kr_w), mw[:, nq + nkv + QK_ROPE:]],
                           axis=1).astype(BF16)
    wq = mla_w_q_up[0].reshape(nq, MH, QK_NOPE + QK_ROPE)
    wq_r = wq[:, :, QK_NOPE:]
    wq2 = jnp.concatenate([wq[:, :, :QK_NOPE].reshape(nq, MH * QK_NOPE),
                           wq_r.reshape(nq, MH * QK_ROPE),
                           rot(wq_r).reshape(nq, MH * QK_ROPE)], axis=1).astype(BF16)
    wkv = mla_w_kv_up[0].reshape(nkv, MH, QK_NOPE + V_HEAD)
    wkn = wkv[:, :, :QK_NOPE].reshape(nkv, MH * QK_NOPE).astype(BF16)
    wvt = wkv[:, :, QK_NOPE:].reshape(nkv, MH * V_HEAD).T.astype(BF16)
    w_out1 = mla_w_out[0].astype(BF16)

    pos = jnp.arange(N_META + T, dtype=F32)
    inv_freq = jnp.exp(-math.log(ROPE_THETA) * jnp.arange(half, dtype=F32) / half)
    ang = pos[:, None] * inv_freq[None, :]
    rc = jnp.tile(jnp.cos(ang), (1, 4))
    rs = jnp.tile(jnp.sin(ang), (1, 4))
    scale = (QK_NOPE + QK_ROPE) ** -0.5 * math.log2(math.e)

    gpre0 = norm_pre[0:1]
    gpre1 = norm_pre[1:2]
    gpost0 = norm_post[0:1]
    gpost1 = norm_post[1:2]

    xm = jnp.concatenate([jnp.zeros((MP - N_META, D), x.dtype), meta_tokens.astype(x.dtype)], axis=0)
    zero8 = jnp.zeros((8, D), x.dtype)
    hp_m, hl_m = _inproj(xm, zero8, gpre0, mu_main, mu_lora, w_main, w_lora, seq=MP, tr=MP)
    s_zero = jnp.zeros((NP, HEAD, LANES), F32)
    y_m, s_meta = _wkv(hp_m, hl_m, par, w2a, s_zero, batch=1, seq=MP, tb=MP)
    h1_m = _outproj(y_m.reshape(MP, DI), hp_m, 3, xm, w_out0, gpost0, tr=MP)
    rc_m = jnp.concatenate([jnp.zeros((MP - N_META, LANES), F32), rc[:N_META]], axis=0)
    rs_m = jnp.concatenate([jnp.zeros((MP - N_META, LANES), F32), rs[:N_META]], axis=0)
    _, _, kn_m, vt_m, kr_m, _ = _mlaproj(h1_m, gpre1, win2, mla_q_norm, wq2, mla_kv_norm, wkn, wvt,
                                         rc_m, rs_m, tr=MP, scale=scale)
    padrows = lambda a: jnp.concatenate(
        [a[MP - N_META:], jnp.zeros((LANES - N_META, a.shape[1]), a.dtype)], axis=0)
    knm, krm = padrows(kn_m), padrows(kr_m)
    vtm = jnp.concatenate([vt_m[:, MP - N_META:], jnp.zeros((DI, LANES - N_META), vt_m.dtype)], axis=1)

    x2 = x.reshape(B * T, D)
    tr_in = _pick(T, (1024, 512, 256, 128, 64))
    hp, hl = _inproj(x2, meta_tokens[N_META - 8:].astype(x.dtype), gpre0, mu_main, mu_lora,
                     w_main, w_lora, seq=T, tr=tr_in)
    y, _ = _wkv(hp, hl, par, w2a, s_meta[0], batch=B, seq=T, tb=_pick(T, (4 * CHUNK, 2 * CHUNK, CHUNK)))
    tr_o = _pick(T, (1024, 512, 256, 128, 64))
    h1 = _outproj(y.reshape(B * T, DI), hp, 3, x2, w_out0, gpost0, tr=tr_o)

    tr_p = _pick(T, (256, 128, 64))
    rc_t = jnp.tile(rc[N_META:], (B, 1))
    rs_t = jnp.tile(rs[N_META:], (B, 1))
    qn, qr, kn, vt, kr, g1 = _mlaproj(h1, gpre1, win2, mla_q_norm, wq2, mla_kv_norm, wkn, wvt,
                                      rc_t, rs_t, tr=tr_p, scale=scale)
    tq = _pick(T, (512, 256, 128))
    o = _attention(qn, qr, kn, kr, vt, knm, krm, vtm, batch=B, seq=T, tq=tq, nmeta=N_META, nh=4)
    out = _outproj(o.reshape(B * T, DI), g1, 0, h1, w_out1, gpost1, tr=tr_o)
    return out.reshape(B, T, D)
```

```python
import functools
import math

import jax
import jax.numpy as jnp
from jax import lax
from jax.experimental import pallas as pl
from jax.experimental.pallas import tpu as pltpu

F32 = jnp.float32
BF16 = jnp.bfloat16

RMS_EPS = 1e-6
GN_EPS = 64e-5
ROPE_THETA = 10000.0
N_META = 16
HEAD = 64
LANES = 128
CHUNK = 64
QK_NOPE = 128
QK_ROPE = 64
V_HEAD = 128
VMEM_LIMIT = 56 * 1024 * 1024


def _cparams(sem):
    return pltpu.CompilerParams(dimension_semantics=sem, vmem_limit_bytes=VMEM_LIMIT)


def _rms(x, g):
    return x * lax.rsqrt(jnp.mean(x * x, axis=-1, keepdims=True) + RMS_EPS) * g


def _dot(a, b):
    return jnp.dot(a.astype(BF16), b.astype(BF16), preferred_element_type=F32)


def _dot_nt(a, b):
    return lax.dot_general(a.astype(BF16), b.astype(BF16), (((1,), (1,)), ((), ())),
                           preferred_element_type=F32)


def _dot_tn(a, b):
    return lax.dot_general(a.astype(BF16), b.astype(BF16), (((0,), (0,)), ((), ())),
                           preferred_element_type=F32)


def _inproj_kernel(x_ref, xprev_ref, x0_ref, g_ref, mu_ref, mul_ref, w_ref, wl_ref,
                   out_ref, outl_ref, u_sc, dx_sc, *, tiles_per_seq):
    i = pl.program_id(0)
    j = pl.program_id(1)

    @pl.when(j == 0)
    def _():
        g = g_ref[...]
        u = _rms(x_ref[...], g)
        first = (i % tiles_per_seq) == 0
        prow = jnp.where(first, x0_ref[7:8, :], xprev_ref[7:8, :])
        up = _rms(prow, g)
        rows = lax.broadcasted_iota(jnp.int32, u.shape, 0)
        ush = jnp.where(rows == 0, up, pltpu.roll(u, 1, 0))
        dx = ush - u
        u_sc[...] = u
        dx_sc[...] = dx
        wl = wl_ref[...]
        ow = _dot(u + mul_ref[0] * dx, wl)
        oa = _dot(u + mul_ref[1] * dx, wl)
        lane = lax.broadcasted_iota(jnp.int32, ow.shape, 1)
        outl_ref[...] = jnp.where(lane < HEAD, ow, oa).astype(outl_ref.dtype)

    xg = u_sc[...] + mu_ref[0] * dx_sc[...]
    out_ref[...] = _dot(xg, w_ref[...]).astype(out_ref.dtype)


def _inproj(x2, x0, g, mu_main, mu_lora, w_main, w_lora, *, seq, tr):
    rows, d = x2.shape
    ncol = w_main.shape[1]
    ngroups = mu_main.shape[0]
    gw = ncol // ngroups
    tpb = tr // 8
    kern = functools.partial(_inproj_kernel, tiles_per_seq=seq // tr)
    return pl.pallas_call(
        kern,
        grid=(rows // tr, ngroups),
        in_specs=[
            pl.BlockSpec((tr, d), lambda i, j: (i, 0)),
            pl.BlockSpec((8, d), lambda i, j: (jnp.maximum(i * tpb - 1, 0), 0)),
            pl.BlockSpec((8, d), lambda i, j: (0, 0)),
            pl.BlockSpec((1, d), lambda i, j: (0, 0)),
            pl.BlockSpec((1, 1, d), lambda i, j: (j, 0, 0)),
            pl.BlockSpec((2, 1, d), lambda i, j: (0, 0, 0)),
            pl.BlockSpec((d, gw), lambda i, j: (0, j)),
            pl.BlockSpec((d, LANES), lambda i, j: (0, 0)),
        ],
        out_specs=[
            pl.BlockSpec((tr, gw), lambda i, j: (i, j)),
            pl.BlockSpec((tr, LANES), lambda i, j: (i, 0)),
        ],
        out_shape=[
            jax.ShapeDtypeStruct((rows, ncol), BF16),
            jax.ShapeDtypeStruct((rows, LANES), BF16),
        ],
        scratch_shapes=[pltpu.VMEM((tr, d), F32), pltpu.VMEM((tr, d), F32)],
        compiler_params=_cparams(("arbitrary", "arbitrary")),
        name="rwkv_inproj",
    )(x2, x2, x0, g, mu_main, mu_lora, w_main, w_lora)


def _bd(x, m0):
    zero = jnp.zeros_like(x)
    return jnp.concatenate([jnp.where(m0, x, zero), jnp.where(m0, zero, x)], axis=0)


def _wkv_kernel(r_ref, k_ref, v_ref, lo_ref, par_ref, w2a_ref, s0_ref, y_ref, sout_ref, s_sc, *, npairs):
    C = CHUNK
    di = npairs * LANES
    nch = r_ref.shape[1] // C

    @pl.when(pl.program_id(1) == 0)
    def _():
        s_sc[...] = s0_ref[...]

    row2 = lax.broadcasted_iota(jnp.int32, (LANES, LANES), 0)
    lane2 = lax.broadcasted_iota(jnp.int32, (LANES, LANES), 1)
    samehead = (row2 // HEAD) == (lane2 // HEAD)
    ones_bd = jnp.where(samehead, 1.0, 0.0).astype(BF16)
    avg_bd = jnp.where(samehead, 1.0 / HEAD, 0.0).astype(BF16)
    P = range(npairs)

    tri =jnp.where(lax.broadcasted_iota(jnp.int32, (C, C), 0)
                    >= lax.broadcasted_iota(jnp.int32, (C, C), 1), 1.0, 0.0).astype(BF16)
    tpos = lax.broadcasted_iota(jnp.int32, (C, LANES), 0)
    lane_c = lax.broadcasted_iota(jnp.int32, (C, LANES), 1)
    spos = lane_c % HEAD
    m0 = lane_c < HEAD
    strict = tpos > spos
    incl = tpos >= spos
    eye = jnp.where(tpos == spos, 1.0, 0.0).astype(F32)
    cat0 = lambda x, y: jnp.concatenate([x, y], axis=0)
    cat1 = lambda x, y: jnp.concatenate([x, y], axis=1)
    b16 = lambda x: x.astype(BF16)

    def head_sum(xs, wmat):
        st = _dot(jnp.concatenate(xs, axis=0), wmat)
        return [st[i * C:(i + 1) * C] for i in range(len(xs))]

    lo = lo_ref[0].astype(F32)
    tl = b16(jnp.where(lax.broadcasted_iota(jnp.int32, lo.shape, 1) < HEAD, jnp.tanh(lo), lo))
    wa_all = [_dot(tl, cat1(w2a_ref[:, p * LANES:(p + 1) * LANES], w2a_ref[:, di + p * LANES:di + (p + 1) * LANES]))
              for p in P]

    def prep(c, pairs, out):
        rows = slice(c * C, (c + 1) * C)
        psl = {p: slice(p * LANES, (p + 1) * LANES) for p in pairs}
        par = lambda n, p: par_ref[n:n + 1, psl[p]]
        wa = {p: wa_all[p][rows] for p in pairs}
        k = {p: k_ref[0, rows, psl[p]].astype(F32) for p in pairs}
        kk = {p: k[p] * par(2, p) for p in pairs}
        n2 = dict(zip(pairs, head_sum([kk[p] * kk[p] for p in pairs], ones_bd)))
        yield
        lw = {p: -math.exp(-0.5) * jax.nn.sigmoid(par(0, p) + wa[p][:, :LANES]) for p in pairs}
        a = {p: jax.nn.sigmoid(par(1, p) + wa[p][:, LANES:]) for p in pairs}
        hi = {p: b16(lw[p]) for p in pairs}
        cs2 = {p: jnp.dot(tri, cat1(hi[p], b16(lw[p] - hi[p].astype(F32))), preferred_element_type=F32)
               for p in pairs}
        yield
        r = {p: r_ref[0, rows, psl[p]].astype(F32) for p in pairs}
        v = {p: v_ref[0, rows, psl[p]].astype(F32) for p in pairs}
        km = {p: k[p] * (1.0 + (a[p] - 1.0) * par(3, p)) for p in pairs}
        bsum = dict(zip(pairs, head_sum([r[p] * km[p] * par(4, p) for p in pairs], ones_bd)))
        yield
        for p in pairs:
            kkn = kk[p] * lax.rsqrt(jnp.maximum(n2[p], 1e-24))
            bv = kkn * a[p]
            cs = cs2[p][:, :LANES] + cs2[p][:, LANES:]
            last = cs[C - 1:C]
            e_out = jnp.exp(-cs)
            e_l = jnp.exp(last - cs)
            rt = r[p] * jnp.exp(cs)
            out[p] = dict(
                wc=jnp.exp(last), bonus=bsum[p] * v[p],
                vb=b16(v[p]), rtb=b16(rt), atb=b16(-kkn * jnp.exp(cs - lw[p])),
                kpb=b16(km[p] * e_l), bpb=b16(bv * e_l),
                zt=cat1(_bd(b16(bv * e_out), m0).T, _bd(b16(km[p] * e_out), m0).T))
        yield

    def staged(fn, out):
        for p in P:
            out.append(fn(p))
            if p % 4 == 3:
                yield

    def indep(d, res):
        sc = []
        yield from staged(lambda p: _dot(cat0(d[p]["atb"], d[p]["rtb"]), d[p]["zt"]), sc)
        ab = [jnp.where(strict, sc[p][:C, :LANES], 0.0) for p in P]
        akrk = [b16(cat0(jnp.where(strict, sc[p][:C, LANES:], 0.0), jnp.where(incl, sc[p][C:, LANES:], 0.0)))
                for p in P]
        res["rbb"] = [b16(jnp.where(incl, sc[p][C:, :LANES], 0.0)) for p in P]

        qb = [b16(ab[p]) for p in P]
        pm = [eye + ab[p] for p in P]
        q = []
        yield from staged(lambda p: _dot(qb[p], _bd(qb[p], m0)), q)
        for _ in range(4):
            qb = [b16(q[p]) for p in P]
            pq = []
            yield from staged(lambda p: _dot(cat0(b16(pm[p]), qb[p]), _bd(qb[p], m0)), pq)
            pm = [pm[p] + pq[p][:C] for p in P]
            q = [pq[p][C:] for p in P]
        res["pm"] = []
        yield from staged(lambda p: b16(pm[p] + _dot(pm[p], _bd(b16(q[p]), m0))), res["pm"])
        res["lv"] = []
        yield from staged(lambda p: _dot(akrk[p], _bd(d[p]["vb"], m0)), res["lv"])

    def dep(c, d, res, state):
        pm, lv, rbb = res["pm"], res["lv"], res["rbb"]
        ars = []
        yield from staged(lambda p: _dot(cat0(d[p]["atb"], d[p]["rtb"]), _bd(b16(state[p]), m0).T), ars)
        ub = []
        yield from staged(lambda p: b16(_dot(pm[p], _bd(b16(ars[p][:C] + lv[p][:C]), m0))), ub)
        ys = []
        yield from staged(lambda p: ars[p][C:] + lv[p][C:] + _dot(rbb[p], _bd(ub[p], m0)), ys)
        sfull = []
        yield from staged(lambda p: _dot(cat0(ub[p], d[p]["vb"]).T, cat0(d[p]["bpb"], d[p]["kpb"])), sfull)
        for p in P:
            state[p] = state[p] * d[p]["wc"] + jnp.where(m0, sfull[p][:HEAD], sfull[p][HEAD:])
        mean = head_sum(ys, avg_bd)
        dd = [ys[p] - mean[p] for p in P]
        yield
        var = head_sum([dd[p] * dd[p] for p in P], avg_bd)
        yield
        for p in P:
            ps = slice(p * LANES, (p + 1) * LANES)
            yn = dd[p] * lax.rsqrt(var[p] + GN_EPS) * par_ref[5:6, ps] + par_ref[6:7, ps] + d[p]["bonus"]
            y_ref[0, c * C:(c + 1) * C, ps] = yn.astype(y_ref.dtype)
            if p % 4 == 3:
                yield

    def prep_all(c, out):
        half = npairs // 2
        yield from prep(c, range(half), out)
        yield from prep(c, range(half, npairs), out)

    def interleave(gens):
        gens = list(gens)
        while gens:
            for g in list(gens):
                if next(g, StopIteration) is StopIteration:
                    gens.remove(g)

    state = [s_sc[p] for p in P]
    prepd = [[None] * npairs for _ in range(nch)]
    res = [dict() for _ in range(nch)]
    interleave([prep_all(0, prepd[0])])
    for ph in range(nch + 1):
        gens = []
        if ph >= 1:
            gens.append(dep(ph - 1, prepd[ph - 1], res[ph - 1], state))
        if ph < nch:
            gens.append(indep(prepd[ph], res[ph]))
        if ph + 1 < nch:
            gens.append(prep_all(ph + 1, prepd[ph + 1]))
        interleave(gens)
    for p in P:
        s_sc[p] = state[p]
        sout_ref[0, p] = state[p]


def _wkv(hp, hl, par, w2a, s0, *, batch, seq, tb):
    d_inner = hp.shape[1] // 4
    npairs = d_inner // LANES
    hp3 = hp.reshape(batch, seq, 4 * d_inner)
    hl3 = hl.reshape(batch, seq, LANES)
    kern = functools.partial(_wkv_kernel, npairs=npairs)
    return pl.pallas_call(
        kern,
        grid=(batch, seq // tb),
        in_specs=[
            pl.BlockSpec((1, tb, d_inner), lambda b, t: (b, t, 0)),
            pl.BlockSpec((1, tb, d_inner), lambda b, t: (b, t, 1)),
            pl.BlockSpec((1, tb, d_inner), lambda b, t: (b, t, 2)),
            pl.BlockSpec((1, tb, LANES), lambda b, t: (b, t, 0)),
            pl.BlockSpec((8, d_inner), lambda b, t: (0, 0)),
            pl.BlockSpec((LANES, 2 * d_inner), lambda b, t: (0, 0)),
            pl.BlockSpec((npairs, HEAD, LANES), lambda b, t: (0, 0, 0)),
        ],
        out_specs=[
            pl.BlockSpec((1, tb, d_inner), lambda b, t: (b, t, 0)),
            pl.BlockSpec((1, npairs, HEAD, LANES), lambda b, t: (b, 0, 0, 0)),
        ],
        out_shape=[
            jax.ShapeDtypeStruct((batch, seq, d_inner), BF16),
            jax.ShapeDtypeStruct((batch, npairs, HEAD, LANES), F32),
        ],
        scratch_shapes=[pltpu.VMEM((npairs, HEAD, LANES), F32)],
        compiler_params=_cparams(("arbitrary", "arbitrary")),
        name="wkv7",
    )(hp3, hp3, hp3, hl3, par, w2a, s0)


def _outproj_kernel(y_ref, g_ref, res_ref, w_ref, gn_ref, out_ref):
    g = g_ref[...].astype(F32)
    z = y_ref[...].astype(F32) * (g * jax.nn.sigmoid(g))
    m = _dot(z, w_ref[...])
    out_ref[...] = res_ref[...] + _rms(m, gn_ref[...])


def _outproj(y2, gsrc, gcol, res2, w, gn, *, tr):
    rows, di = y2.shape
    d = w.shape[1]
    return pl.pallas_call(
        _outproj_kernel,
        grid=(rows // tr,),
        in_specs=[
            pl.BlockSpec((tr, di), lambda i: (i, 0)),
            pl.BlockSpec((tr, di), lambda i: (i, gcol)),
            pl.BlockSpec((tr, d), lambda i: (i, 0)),
            pl.BlockSpec((di, d), lambda i: (0, 0)),
            pl.BlockSpec((1, d), lambda i: (0, 0)),
        ],
        out_specs=pl.BlockSpec((tr, d), lambda i: (i, 0)),
        out_shape=jax.ShapeDtypeStruct((rows, d), F32),
        compiler_params=_cparams(("arbitrary",)),
        name="gate_outproj",
    )(y2, gsrc, res2, w, gn)


def _mlaproj_kernel(h_ref, gpre_ref, win_ref, qn_ref, wq_ref, kvn_ref, wkn_ref, wvt_ref, rc_ref, rs_ref,
                    oqn_ref, oqr_ref, okn_ref, ovt_ref, okr_ref, og_ref, *, scale):
    u = _rms(h_ref[...], gpre_ref[...])
    t = _dot(u, win_ref[...])
    nq = qn_ref.shape[1]
    nkv = kvn_ref.shape[1]
    di = og_ref.shape[1]
    c_q = t[:, :nq]
    c_kv = t[:, nq:nq + nkv]
    krr = t[:, nq + nkv:nq + nkv + LANES]
    og_ref[...] = t[:, nq + nkv + LANES:].astype(og_ref.dtype)

    rc = rc_ref[...]
    rs = rs_ref[...]
    lane = lax.broadcasted_iota(jnp.int32, krr.shape, 1)
    prod = krr * jnp.where(lane < QK_ROPE, rc, rs)
    okr_ref[...] = (prod + pltpu.roll(prod, QK_ROPE, 1)).astype(okr_ref.dtype)

    q = _dot(_rms(c_q, qn_ref[...]), wq_ref[...]) * scale
    oqn_ref[...] = q[:, :di].astype(oqn_ref.dtype)
    nrep = (di // 2) // LANES
    rct = jnp.concatenate([rc] * nrep, axis=1)
    rst = jnp.concatenate([rs] * nrep, axis=1)
    oqr_ref[...] = (q[:, di:di + di // 2] * rct + q[:, di + di // 2:] * rst).astype(oqr_ref.dtype)

    ckv = _rms(c_kv, kvn_ref[...])
    okn_ref[...] = _dot(ckv, wkn_ref[...]).astype(okn_ref.dtype)
    ovt_ref[...] = _dot_nt(wvt_ref[...], ckv).astype(ovt_ref.dtype)


def _mlaproj(h2, gpre, win, qn, wq, kvn, wkn, wvt, rc, rs, *, tr, scale):
    rows, d = h2.shape
    di = wkn.shape[1]
    full = lambda a: pl.BlockSpec(a.shape, lambda i: (0, 0))
    rowblk = lambda n: pl.BlockSpec((tr, n), lambda i: (i, 0))
    tps = rc.shape[0] // tr
    rope = pl.BlockSpec((tr, LANES), lambda i: (i % tps, 0))
    kern = functools.partial(_mlaproj_kernel, scale=scale)
    return pl.pallas_call(
        kern,
        grid=(rows // tr,),
        in_specs=[rowblk(d), full(gpre), full(win), full(qn), full(wq), full(kvn), full(wkn), full(wvt),
                  rope, rope],
        out_specs=[rowblk(di), rowblk(di // 2), rowblk(di), pl.BlockSpec((di, tr), lambda i: (0, i)),
                   rowblk(LANES), rowblk(di)],
        out_shape=[
            jax.ShapeDtypeStruct((rows, di), BF16),
            jax.ShapeDtypeStruct((rows, di // 2), BF16),
            jax.ShapeDtypeStruct((rows, di), BF16),
            jax.ShapeDtypeStruct((di, rows), BF16),
            jax.ShapeDtypeStruct((rows, LANES), BF16),
            jax.ShapeDtypeStruct((rows, di), BF16),
        ],
        compiler_params=_cparams(("arbitrary",)),
        name="mla_proj",
    )(h2, gpre, win, qn, wq, kvn, wkn, wvt, rc, rs)


NEG = -1e30


def _attn_kernel(qn_ref, qr_ref, kn_ref, kr_ref, vt_ref, knm_ref, krm_ref, vtm_ref, o_ref, s_sc, qt_sc, *,
                 tq, nmeta):
    qi = pl.program_id(2)
    mp = knm_ref.shape[1]
    nh = qn_ref.shape[2] // LANES
    H = range(nh)
    lane = lax.broadcasted_iota(jnp.int32, (tq, LANES), 1)
    hsl = [slice(h * LANES, (h + 1) * LANES) for h in H]

    def query(h):
        qr = qr_ref[0, :, hsl[h // 2]]
        return jnp.concatenate([qn_ref[0, :, hsl[h]],
                                jnp.where((lane // QK_ROPE) == h % 2, qr, jnp.zeros_like(qr))], axis=1)

    for h in H:
        qt_sc[h] = query(h).T

    def scores(j, h):
        rows = pl.ds(pl.multiple_of(j * tq, tq), tq)
        return _dot(jnp.concatenate([kn_ref[0, rows, hsl[h]], kr_ref[0, rows, :]], axis=1), qt_sc[h])

    def update(carry, s, smax, vt):
        m, acc = carry
        mn = jnp.maximum(m, smax)
        p = jnp.exp2(s - mn)
        vta = jnp.concatenate([vt, jnp.ones((16, vt.shape[1]), BF16)], axis=0)
        return [mn, jnp.exp2(m - mn) * acc + _dot(vta, p)]

    colmax = lambda s: jnp.max(s, axis=0, keepdims=True)

    bmax = []
    for h in H:
        s0 = scores(0, h)
        s_sc[h] = s0
        bmax.append(colmax(s0))

    def step(j, carry):
        rows = pl.ds(pl.multiple_of(j * tq, tq), tq)
        out = []
        for h in H:
            nxt = scores(j + 1, h)
            out += update(carry[3 * h:3 * h + 2], s_sc[h], carry[3 * h + 2], vt_ref[0, hsl[h], rows])
            s_sc[h] = nxt
            out.append(colmax(nxt))
        return out

    row1 = lambda val: jnp.full((1, tq), val, F32)
    carry = []
    for h in H:
        carry += [row1(NEG), jnp.zeros((LANES + 16, tq), F32), bmax[h]]
    carry = lax.fori_loop(0, qi, step, carry)

    krow = lax.broadcasted_iota(jnp.int32, (tq, tq), 0)
    qcol = lax.broadcasted_iota(jnp.int32, (tq, tq), 1)
    causal = krow <= qcol
    drows = pl.ds(pl.multiple_of(qi * tq, tq), tq)
    sm = [_dot(jnp.concatenate([knm_ref[0, :, hsl[h]], krm_ref[0]], axis=1), qt_sc[h])[:nmeta] for h in H]
    pad = jnp.zeros((mp - nmeta, tq), F32)
    for h in H:
        sd = jnp.where(causal, s_sc[h], NEG)
        m, acc = carry[3 * h:3 * h + 2]
        mn = jnp.maximum(m, jnp.maximum(colmax(sm[h]), colmax(sd)))
        p = jnp.concatenate([jnp.exp2(sm[h] - mn), pad, jnp.exp2(sd - mn)], axis=0)
        vta = jnp.concatenate([jnp.concatenate([vtm_ref[0, hsl[h], :], vt_ref[0, hsl[h], drows]], axis=1),
                               jnp.ones((16, mp + tq), BF16)], axis=0)
        acc = jnp.exp2(m - mn) * acc + _dot(vta, p)
        o_ref[0, :, hsl[h]] = (acc[:LANES] / acc[LANES:LANES + 1]).T.astype(o_ref.dtype)


def _attention(qn, qr, kn, kr, vt, knm, krm, vtm, *, batch, seq, tq, nmeta, nh):
    di = qn.shape[1]
    hw = nh * LANES
    r3 = lambda a: a.reshape(batch, seq, a.shape[1])
    mp = knm.shape[0]
    kern = functools.partial(_attn_kernel, tq=tq, nmeta=nmeta)
    return pl.pallas_call(
        kern,
        grid=(batch, di // hw, seq // tq),
        in_specs=[
            pl.BlockSpec((1, tq, hw), lambda b, p, i: (b, i, p)),
            pl.BlockSpec((1, tq, hw // 2), lambda b, p, i: (b, i, p)),
            pl.BlockSpec((1, seq, hw), lambda b, p, i: (b, 0, p)),
            pl.BlockSpec((1, seq, LANES), lambda b, p, i: (b, 0, 0)),
            pl.BlockSpec((1, hw, seq), lambda b, p, i: (0, p, b)),
            pl.BlockSpec((1, mp, hw), lambda b, p, i: (0, 0, p)),
            pl.BlockSpec((1, mp, LANES), lambda b, p, i: (0, 0, 0)),
            pl.BlockSpec((1, hw, mp), lambda b, p, i: (0, p, 0)),
        ],
        out_specs=pl.BlockSpec((1, tq, hw), lambda b, p, i: (b, i, p)),
        out_shape=jax.ShapeDtypeStruct((batch, seq, di), BF16),
        scratch_shapes=[pltpu.VMEM((nh, tq, tq), F32), pltpu.VMEM((nh, 2 * LANES, tq), BF16)],
        compiler_params=_cparams(("arbitrary", "arbitrary", "arbitrary")),
        name="mla_attention",
    )(r3(qn), r3(qr), r3(kn), r3(kr), vt[None], knm[None], krm[None], vtm[None])


def _pick(n, cands):
    for c in cands:
        if n % c == 0:
            return c
    return n


def kernel(x, meta_tokens, norm_pre, norm_post, rwkv_mu, rwkv_w_in, rwkv_w0, rwkv_w2, rwkv_a0, rwkv_a2,
           rwkv_k_k, rwkv_k_a, rwkv_r_k, rwkv_ln_w, rwkv_ln_b, rwkv_w_out,
           mla_w_in, mla_q_norm, mla_w_q_up, mla_kv_norm, mla_w_kv_up, mla_w_out):
    B, T, D = x.shape
    DI = rwkv_w_out.shape[1]
    NP = DI // LANES
    MH = DI // V_HEAD
    nq = mla_q_norm.shape[1]
    nkv = mla_kv_norm.shape[1]
    MP = CHUNK
    assert meta_tokens.shape[0] == N_META and T % CHUNK == 0

    w_in = rwkv_w_in[0]
    w_main = w_in[:, :4 * DI].astype(BF16)
    w_lora = w_in[:, 4 * DI:].astype(BF16)
    mu_main = rwkv_mu[0, :4].reshape(4, 1, D)
    mu_lora = rwkv_mu[0, 4:].reshape(2, 1, D)
    zeros1 = jnp.zeros((DI,), F32)
    par = jnp.stack([rwkv_w0[0], rwkv_a0[0], rwkv_k_k[0], rwkv_k_a[0], rwkv_r_k[0],
                     rwkv_ln_w[0], rwkv_ln_b[0], zeros1])
    zblk = jnp.zeros_like(rwkv_w2[0])
    w2a = jnp.concatenate([jnp.concatenate([rwkv_w2[0], zblk], axis=1),
                           jnp.concatenate([zblk, rwkv_a2[0]], axis=1)], axis=0).astype(BF16)
    w_out0 = rwkv_w_out[0].astype(BF16)

    mw = mla_w_in[0]
    kr_w = mw[:, nq + nkv:nq + nkv + QK_ROPE]
    half = QK_ROPE // 2
    rot = lambda wr: jnp.concatenate([-wr[..., half:], wr[..., :half]], axis=-1)
    win2 = jnp.concatenate([mw[:, :nq + nkv], kr_w, rot(kr_w), mw[:, nq + nkv + QK_ROPE:]],
                           axis=1).astype(BF16)
    wq = mla_w_q_up[0].reshape(nq, MH, QK_NOPE + QK_ROPE)
    wq_r = wq[:, :, QK_NOPE:]
    wq2 = jnp.concatenate([wq[:, :, :QK_NOPE].reshape(nq, MH * QK_NOPE),
                           wq_r.reshape(nq, MH * QK_ROPE),
                           rot(wq_r).reshape(nq, MH * QK_ROPE)], axis=1).astype(BF16)
    wkv = mla_w_kv_up[0].reshape(nkv, MH, QK_NOPE + V_HEAD)
    wkn = wkv[:, :, :QK_NOPE].reshape(nkv, MH * QK_NOPE).astype(BF16)
    wvt = wkv[:, :, QK_NOPE:].reshape(nkv, MH * V_HEAD).T.astype(BF16)
    w_out1 = mla_w_out[0].astype(BF16)

    pos = jnp.arange(N_META + T, dtype=F32)
    inv_freq = jnp.exp(-math.log(ROPE_THETA) * jnp.arange(half, dtype=F32) / half)
    ang = pos[:, None] * inv_freq[None, :]
    rc = jnp.tile(jnp.cos(ang), (1, 4))
    rs = jnp.tile(jnp.sin(ang), (1, 4))
    scale = (QK_NOPE + QK_ROPE) ** -0.5 * math.log2(math.e)

    gpre0 = norm_pre[0:1]
    gpre1 = norm_pre[1:2]
    gpost0 = norm_post[0:1]
    gpost1 = norm_post[1:2]

    xm = jnp.concatenate([jnp.zeros((MP - N_META, D), x.dtype), meta_tokens.astype(x.dtype)], axis=0)
    zero8 = jnp.zeros((8, D), x.dtype)
    hp_m, hl_m = _inproj(xm, zero8, gpre0, mu_main, mu_lora, w_main, w_lora, seq=MP, tr=MP)
    s_zero = jnp.zeros((NP, HEAD, LANES), F32)
    y_m, s_meta = _wkv(hp_m, hl_m, par, w2a, s_zero, batch=1, seq=MP, tb=MP)
    h1_m = _outproj(y_m.reshape(MP, DI), hp_m, 3, xm, w_out0, gpost0, tr=MP)
    rc_m = jnp.concatenate([jnp.zeros((MP - N_META, LANES), F32), rc[:N_META]], axis=0)
    rs_m = jnp.concatenate([jnp.zeros((MP - N_META, LANES), F32), rs[:N_META]], axis=0)
    _, _, kn_m, vt_m, kr_m, _ = _mlaproj(h1_m, gpre1, win2, mla_q_norm, wq2, mla_kv_norm, wkn, wvt,
                                         rc_m, rs_m, tr=MP, scale=scale)
    padrows = lambda a: jnp.concatenate(
        [a[MP - N_META:], jnp.zeros((LANES - N_META, a.shape[1]), a.dtype)], axis=0)
    knm, krm = padrows(kn_m), padrows(kr_m)
    vtm = jnp.concatenate([vt_m[:, MP - N_META:], jnp.zeros((DI, LANES - N_META), vt_m.dtype)], axis=1)

    x2 = x.reshape(B * T, D)
    tr_in = _pick(T, (1024, 512, 256, 128, 64))
    hp, hl = _inproj(x2, meta_tokens[N_META - 8:].astype(x.dtype), gpre0, mu_main, mu_lora,
                     w_main, w_lora, seq=T, tr=tr_in)
    y, _ = _wkv(hp, hl, par, w2a, s_meta[0], batch=B, seq=T, tb=_pick(T, (4 * CHUNK, 2 * CHUNK, CHUNK)))
    tr_o = _pick(T, (1024, 512, 256, 128, 64))
    h1 = _outproj(y.reshape(B * T, DI), hp, 3, x2, w_out0, gpost0, tr=tr_o)

    tr_p = _pick(T, (512, 256, 128, 64))
    qn, qr, kn, vt, kr, g1 = _mlaproj(h1, gpre1, win2, mla_q_norm, wq2, mla_kv_norm, wkn, wvt,
                                      rc[N_META:], rs[N_META:], tr=tr_p, scale=scale)
    tq = _pick(T, (512, 256, 128))
    o = _attention(qn, qr, kn, kr, vt, knm, krm, vtm, batch=B, seq=T, tq=tq, nmeta=N_META, nh=4)
    out = _outproj(o.reshape(B * T, DI), g1, 0, h1, w_out1, gpost1, tr=tr_o)
    return out.reshape(B, T, D)
```

```python
import functools
import math

import jax
import jax.numpy as jnp
from jax import lax
from jax.experimental import pallas as pl
from jax.experimental.pallas import tpu as pltpu

F32 = jnp.float32
BF16 = jnp.bfloat16

RMS_EPS = 1e-6
GN_EPS = 64e-5
ROPE_THETA = 10000.0
N_META = 16
HEAD = 64
LANES = 128
CHUNK = 64
QK_NOPE = 128
QK_ROPE = 64
V_HEAD = 128
VMEM_LIMIT = 56 * 1024 * 1024


def _cparams(sem):
    return pltpu.CompilerParams(dimension_semantics=sem, vmem_limit_bytes=VMEM_LIMIT)


def _rms(x, g):
    return x * lax.rsqrt(jnp.mean(x * x, axis=-1, keepdims=True) + RMS_EPS) * g


def _dot(a, b):
    return jnp.dot(a.astype(BF16), b.astype(BF16), preferred_element_type=F32)


def _dot_nt(a, b):
    return lax.dot_general(a.astype(BF16), b.astype(BF16), (((1,), (1,)), ((), ())),
                           preferred_element_type=F32)


def _dot_tn(a, b):
    return lax.dot_general(a.astype(BF16), b.astype(BF16), (((0,), (0,)), ((), ())),
                           preferred_element_type=F32)


def _inproj_kernel(x_ref, xprev_ref, x0_ref, g_ref, mu_ref, mul_ref, w_ref, wl_ref,
                   out_ref, outl_ref, u_sc, dx_sc, *, tiles_per_seq):
    i = pl.program_id(0)
    j = pl.program_id(1)

    @pl.when(j == 0)
    def _():
        g = g_ref[...]
        u = _rms(x_ref[...], g)
        first = (i % tiles_per_seq) == 0
        prow = jnp.where(first, x0_ref[7:8, :], xprev_ref[7:8, :])
        up = _rms(prow, g)
        rows = lax.broadcasted_iota(jnp.int32, u.shape, 0)
        ush = jnp.where(rows == 0, up, pltpu.roll(u, 1, 0))
        dx = ush - u
        u_sc[...] = u
        dx_sc[...] = dx
        wl = wl_ref[...]
        ow = _dot(u + mul_ref[0] * dx, wl)
        oa = _dot(u + mul_ref[1] * dx, wl)
        lane = lax.broadcasted_iota(jnp.int32, ow.shape, 1)
        outl_ref[...] = jnp.where(lane < HEAD, ow, oa).astype(outl_ref.dtype)

    xg = u_sc[...] + mu_ref[0] * dx_sc[...]
    out_ref[...] = _dot(xg, w_ref[...]).astype(out_ref.dtype)


def _inproj(x2, x0, g, mu_main, mu_lora, w_main, w_lora, *, seq, tr):
    rows, d = x2.shape
    ncol = w_main.shape[1]
    ngroups = mu_main.shape[0]
    gw = ncol // ngroups
    tpb = tr // 8
    kern = functools.partial(_inproj_kernel, tiles_per_seq=seq // tr)
    return pl.pallas_call(
        kern,
        grid=(rows // tr, ngroups),
        in_specs=[
            pl.BlockSpec((tr, d), lambda i, j: (i, 0)),
            pl.BlockSpec((8, d), lambda i, j: (jnp.maximum(i * tpb - 1, 0), 0)),
            pl.BlockSpec((8, d), lambda i, j: (0, 0)),
            pl.BlockSpec((1, d), lambda i, j: (0, 0)),
            pl.BlockSpec((1, 1, d), lambda i, j: (j, 0, 0)),
            pl.BlockSpec((2, 1, d), lambda i, j: (0, 0, 0)),
            pl.BlockSpec((d, gw), lambda i, j: (0, j)),
            pl.BlockSpec((d, LANES), lambda i, j: (0, 0)),
        ],
        out_specs=[
            pl.BlockSpec((tr, gw), lambda i, j: (i, j)),
            pl.BlockSpec((tr, LANES), lambda i, j: (i, 0)),
        ],
        out_shape=[
            jax.ShapeDtypeStruct((rows, ncol), BF16),
            jax.ShapeDtypeStruct((rows, LANES), BF16),
        ],
        scratch_shapes=[pltpu.VMEM((tr, d), F32), pltpu.VMEM((tr, d), F32)],
        compiler_params=_cparams(("arbitrary", "arbitrary")),
        name="rwkv_inproj",
    )(x2, x2, x0, g, mu_main, mu_lora, w_main, w_lora)


def _bd(x, m0):
    zero = jnp.zeros_like(x)
    return jnp.concatenate([jnp.where(m0, x, zero), jnp.where(m0, zero, x)], axis=0)


def _wkv_kernel(r_ref, k_ref, v_ref, lo_ref, par_ref, w2a_ref, s0_ref, y_ref, sout_ref, s_sc, *, npairs):
    C = CHUNK
    di = npairs * LANES
    nch = r_ref.shape[1] // C

    @pl.when(pl.program_id(1) == 0)
    def _():
        s_sc[...] = s0_ref[...]

    row2 = lax.broadcasted_iota(jnp.int32, (LANES, LANES), 0)
    lane2 = lax.broadcasted_iota(jnp.int32, (LANES, LANES), 1)
    ones_bd = jnp.where((row2 // HEAD) == (lane2 // HEAD), 1.0, 0.0).astype(BF16)
    P = range(npairs)

    tri = jnp.where(lax.broadcasted_iota(jnp.int32, (C, C), 0)
                    >= lax.broadcasted_iota(jnp.int32, (C, C), 1), 1.0, 0.0).astype(BF16)
    tpos = lax.broadcasted_iota(jnp.int32, (C, LANES), 0)
    lane_c = lax.broadcasted_iota(jnp.int32, (C, LANES), 1)
    spos = lane_c % HEAD
    m0 = lane_c < HEAD
    strict = tpos > spos
    incl = tpos >= spos
    eye = jnp.where(tpos == spos, 1.0, 0.0).astype(F32)
    cat0 = lambda x, y: jnp.concatenate([x, y], axis=0)
    cat1 = lambda x, y: jnp.concatenate([x, y], axis=1)
    b16 = lambda x: x.astype(BF16)

    def head_sum_mxu(xs):
        st = _dot(jnp.concatenate(xs, axis=0), ones_bd)
        return [st[i * C:(i + 1) * C] for i in range(len(xs))]

    def head_sum(xs):
        out = []
        for x in xs:
            s0 = jnp.sum(jnp.where(m0, x, 0.0), axis=1, keepdims=True)
            s1 = jnp.sum(jnp.where(m0, 0.0, x), axis=1, keepdims=True)
            out.append(jnp.where(m0, s0, s1))
        return out

    lo = lo_ref[0].astype(F32)
    tl = b16(jnp.where(lax.broadcasted_iota(jnp.int32, lo.shape, 1) < HEAD, jnp.tanh(lo), lo))
    wa_all = [_dot(tl, cat1(w2a_ref[:, p * LANES:(p + 1) * LANES], w2a_ref[:, di + p * LANES:di + (p + 1) * LANES]))
              for p in P]

    def prep(c, pairs, out):
        rows = slice(c * C, (c + 1) * C)
        psl = {p: slice(p * LANES, (p + 1) * LANES) for p in pairs}
        par = lambda n, p: par_ref[n:n + 1, psl[p]]
        wa = {p: wa_all[p][rows] for p in pairs}
        k = {p: k_ref[0, rows, psl[p]].astype(F32) for p in pairs}
        kk = {p: k[p] * par(2, p) for p in pairs}
        n2 = dict(zip(pairs, head_sum_mxu([kk[p] * kk[p] for p in pairs])))
        yield
        lw = {p: -math.exp(-0.5) * jax.nn.sigmoid(par(0, p) + wa[p][:, :LANES]) for p in pairs}
        a = {p: jax.nn.sigmoid(par(1, p) + wa[p][:, LANES:]) for p in pairs}
        hi = {p: b16(lw[p]) for p in pairs}
        cs2 = {p: jnp.dot(tri, cat1(hi[p], b16(lw[p] - hi[p].astype(F32))), preferred_element_type=F32)
               for p in pairs}
        yield
        r = {p: r_ref[0, rows, psl[p]].astype(F32) for p in pairs}
        v = {p: v_ref[0, rows, psl[p]].astype(F32) for p in pairs}
        km = {p: k[p] * (1.0 + (a[p] - 1.0) * par(3, p)) for p in pairs}
        bsum = dict(zip(pairs, head_sum_mxu([r[p] * km[p] * par(4, p) for p in pairs])))
        yield
        for p in pairs:
            kkn = kk[p] * lax.rsqrt(jnp.maximum(n2[p], 1e-24))
            bv = kkn * a[p]
            cs = cs2[p][:, :LANES] + cs2[p][:, LANES:]
            last = cs[C - 1:C]
            e_out = jnp.exp(-cs)
            e_l = jnp.exp(last - cs)
            rt = r[p] * jnp.exp(cs)
            out[p] = dict(
                wc=jnp.exp(last), bonus=bsum[p] * v[p],
                vb=b16(v[p]), rtb=b16(rt), atb=b16(-kkn * jnp.exp(cs - lw[p])),
                kpb=b16(km[p] * e_l), bpb=b16(bv * e_l),
                zt=cat1(_bd(b16(bv * e_out), m0).T, _bd(b16(km[p] * e_out), m0).T))
        yield

    def staged(fn, out):
        for p in P:
            out.append(fn(p))
            if p % 4 == 3:
                yield

    def indep(d, res):
        sc = []
        yield from staged(lambda p: _dot(cat0(d[p]["atb"], d[p]["rtb"]), d[p]["zt"]), sc)
        ab = [jnp.where(strict, sc[p][:C, :LANES], 0.0) for p in P]
        akrk = [b16(cat0(jnp.where(strict, sc[p][:C, LANES:], 0.0), jnp.where(incl, sc[p][C:, LANES:], 0.0)))
                for p in P]
        res["rbb"] = [b16(jnp.where(incl, sc[p][C:, :LANES], 0.0)) for p in P]

        qb = [b16(ab[p]) for p in P]
        pm = [eye + ab[p] for p in P]
        q = []
        yield from staged(lambda p: _dot(qb[p], _bd(qb[p], m0)), q)
        for _ in range(4):
            qb = [b16(q[p]) for p in P]
            pq = []
            yield from staged(lambda p: _dot(cat0(b16(pm[p]), qb[p]), _bd(qb[p], m0)), pq)
            pm = [pm[p] + pq[p][:C] for p in P]
            q = [pq[p][C:] for p in P]
        res["pm"] = []
        yield from staged(lambda p: b16(pm[p] + _dot(pm[p], _bd(b16(q[p]), m0))), res["pm"])
        res["lv"] = []
        yield from staged(lambda p: _dot(akrk[p], _bd(d[p]["vb"], m0)), res["lv"])

    def dep(c, d, res, state):
        pm, lv, rbb = res["pm"], res["lv"], res["rbb"]
        ars = []
        def sbt(p):
            t = _bd(b16(state[p]), m0).T
            return cat0(t[:HEAD], t[HEAD:])
        yield from staged(lambda p: _dot(cat0(d[p]["atb"], d[p]["rtb"]), sbt(p)), ars)
        ub = []
        yield from staged(lambda p: b16(_dot(pm[p], _bd(b16(ars[p][:C] + lv[p][:C]), m0))), ub)
        ys = []
        yield from staged(lambda p: ars[p][C:] + lv[p][C:] + _dot(rbb[p], _bd(ub[p], m0)), ys)
        sfull = []
        yield from staged(lambda p: _dot(cat0(ub[p], d[p]["vb"]).T, cat0(d[p]["bpb"], d[p]["kpb"])), sfull)
        for p in P:
            state[p] = state[p] * d[p]["wc"] + jnp.where(m0, sfull[p][:HEAD], sfull[p][HEAD:])
        mean = head_sum(ys)
        dd = [ys[p] - mean[p] * (1.0 / HEAD) for p in P]
        yield
        var = [v_ * (1.0 / HEAD) for v_ in head_sum([dd[p] * dd[p] for p in P])]
        yield
        for p in P:
            ps = slice(p * LANES, (p + 1) * LANES)
            yn = dd[p] * lax.rsqrt(var[p] + GN_EPS) * par_ref[5:6, ps] + par_ref[6:7, ps] + d[p]["bonus"]
            y_ref[0, c * C:(c + 1) * C, ps] = yn.astype(y_ref.dtype)
            if p % 4 == 3:
                yield

    def prep_all(c, out):
        half = npairs // 2
        yield from prep(c, range(half), out)
        yield from prep(c, range(half, npairs), out)

    def interleave(gens):
        gens = list(gens)
        while gens:
            for g in list(gens):
                if next(g, StopIteration) is StopIteration:
                    gens.remove(g)

    state = [s_sc[p] for p in P]
    prepd = [[None] * npairs for _ in range(nch)]
    res = [dict() for _ in range(nch)]
    interleave([prep_all(0, prepd[0])])
    for ph in range(nch + 1):
        gens = []
        if ph >= 1:
            gens.append(dep(ph - 1, prepd[ph - 1], res[ph - 1], state))
        if ph < nch:
            gens.append(indep(prepd[ph], res[ph]))
        if ph + 1 < nch:
            gens.append(prep_all(ph + 1, prepd[ph + 1]))
        interleave(gens)
    for p in P:
        s_sc[p] = state[p]
        sout_ref[0, p] = state[p]


def _wkv(hp, hl, par, w2a, s0, *, batch, seq, tb):
    d_inner = hp.shape[1] // 4
    npairs = d_inner // LANES
    hp3 = hp.reshape(batch, seq, 4 * d_inner)
    hl3 = hl.reshape(batch, seq, LANES)
    kern = functools.partial(_wkv_kernel, npairs=npairs)
    return pl.pallas_call(
        kern,
        grid=(batch, seq // tb),
        in_specs=[
            pl.BlockSpec((1, tb, d_inner), lambda b, t: (b, t, 0)),
            pl.BlockSpec((1, tb, d_inner), lambda b, t: (b, t, 1)),
            pl.BlockSpec((1, tb, d_inner), lambda b, t: (b, t, 2)),
            pl.BlockSpec((1, tb, LANES), lambda b, t: (b, t, 0)),
            pl.BlockSpec((8, d_inner), lambda b, t: (0, 0)),
            pl.BlockSpec((LANES, 2 * d_inner), lambda b, t: (0, 0)),
            pl.BlockSpec((npairs, HEAD, LANES), lambda b, t: (0, 0, 0)),
        ],
        out_specs=[
            pl.BlockSpec((1, tb, d_inner), lambda b, t: (b, t, 0)),
            pl.BlockSpec((1, npairs, HEAD, LANES), lambda b, t: (b, 0, 0, 0)),
        ],
        out_shape=[
            jax.ShapeDtypeStruct((batch, seq, d_inner), BF16),
            jax.ShapeDtypeStruct((batch, npairs, HEAD, LANES), F32),
        ],
        scratch_shapes=[pltpu.VMEM((npairs, HEAD, LANES), F32)],
        compiler_params=_cparams(("arbitrary", "arbitrary")),
        name="wkv7",
    )(hp3, hp3, hp3, hl3, par, w2a, s0)


def _outproj_kernel(y_ref, g_ref, res_ref, w_ref, gn_ref, out_ref):
    g = g_ref[...].astype(F32)
    z = y_ref[...].astype(F32) * (g * jax.nn.sigmoid(g))
    m = _dot(z, w_ref[...])
    out_ref[...] = res_ref[...] + _rms(m, gn_ref[...])


def _outproj(y2, gsrc, gcol, res2, w, gn, *, tr):
    rows, di = y2.shape
    d = w.shape[1]
    return pl.pallas_call(
        _outproj_kernel,
        grid=(rows // tr,),
        in_specs=[
            pl.BlockSpec((tr, di), lambda i: (i, 0)),
            pl.BlockSpec((tr, di), lambda i: (i, gcol)),
            pl.BlockSpec((tr, d), lambda i: (i, 0)),
            pl.BlockSpec((di, d), lambda i: (0, 0)),
            pl.BlockSpec((1, d), lambda i: (0, 0)),
        ],
        out_specs=pl.BlockSpec((tr, d), lambda i: (i, 0)),
        out_shape=jax.ShapeDtypeStruct((rows, d), F32),
        compiler_params=_cparams(("arbitrary",)),
        name="gate_outproj",
    )(y2, gsrc, res2, w, gn)


def _mlaproj_kernel(h_ref, gpre_ref, win_ref, qn_ref, wq_ref, kvn_ref, wkn_ref, wvt_ref, rc_ref, rs_ref,
                    oqn_ref, oqr_ref, okn_ref, ovt_ref, okr_ref, og_ref, *, scale):
    u = _rms(h_ref[...], gpre_ref[...])
    t = _dot(u, win_ref[...])
    nq = qn_ref.shape[1]
    nkv = kvn_ref.shape[1]
    di = og_ref.shape[1]
    c_q = t[:, :nq]
    c_kv = t[:, nq:nq + nkv]
    krr = t[:, nq + nkv:nq + nkv + LANES]
    og_ref[...] = t[:, nq + nkv + LANES:].astype(og_ref.dtype)

    rc = rc_ref[...]
    rs = rs_ref[...]
    lane = lax.broadcasted_iota(jnp.int32, krr.shape, 1)
    prod = krr * jnp.where(lane < QK_ROPE, rc, rs)
    okr_ref[...] = (prod + pltpu.roll(prod, QK_ROPE, 1)).astype(okr_ref.dtype)

    q = _dot(_rms(c_q, qn_ref[...]), wq_ref[...]) * scale
    oqn_ref[...] = q[:, :di].astype(oqn_ref.dtype)
    nrep = (di // 2) // LANES
    rct = jnp.concatenate([rc] * nrep, axis=1)
    rst = jnp.concatenate([rs] * nrep, axis=1)
    oqr_ref[...] = (q[:, di:di + di // 2] * rct + q[:, di + di // 2:] * rst).astype(oqr_ref.dtype)

    ckv = _rms(c_kv, kvn_ref[...])
    okn_ref[...] = _dot(ckv, wkn_ref[...]).astype(okn_ref.dtype)
    ovt_ref[...] = _dot_nt(wvt_ref[...], ckv).astype(ovt_ref.dtype)


def _mlaproj(h2, gpre, win, qn, wq, kvn, wkn, wvt, rc, rs, *, tr, scale):
    rows, d = h2.shape
    di = wkn.shape[1]
    full = lambda a: pl.BlockSpec(a.shape, lambda i: (0, 0))
    rowblk = lambda n: pl.BlockSpec((tr, n), lambda i: (i, 0))
    tps = rc.shape[0] // tr
    rope = pl.BlockSpec((tr, LANES), lambda i: (i % tps, 0))
    kern = functools.partial(_mlaproj_kernel, scale=scale)
    return pl.pallas_call(
        kern,
        grid=(rows // tr,),
        in_specs=[rowblk(d), full(gpre), full(win), full(qn), full(wq), full(kvn), full(wkn), full(wvt),
                  rope, rope],
        out_specs=[rowblk(di), rowblk(di // 2), rowblk(di), pl.BlockSpec((di, tr), lambda i: (0, i)),
                   rowblk(LANES), rowblk(di)],
        out_shape=[
            jax.ShapeDtypeStruct((rows, di), BF16),
            jax.ShapeDtypeStruct((rows, di // 2), BF16),
            jax.ShapeDtypeStruct((rows, di), BF16),
            jax.ShapeDtypeStruct((di, rows), BF16),
            jax.ShapeDtypeStruct((rows, LANES), BF16),
            jax.ShapeDtypeStruct((rows, di), BF16),
        ],
        compiler_params=_cparams(("arbitrary",)),
        name="mla_proj",
    )(h2, gpre, win, qn, wq, kvn, wkn, wvt, rc, rs)


NEG = -1e30


def _attn_kernel(qn_ref, qr_ref, kn_ref, kr_ref, vt_ref, knm_ref, krm_ref, vtm_ref, o_ref, s_sc, qt_sc, *,
                 tq, nmeta):
    qi = pl.program_id(2)
    mp = knm_ref.shape[1]
    nh = qn_ref.shape[2] // LANES
    H = range(nh)
    lane = lax.broadcasted_iota(jnp.int32, (tq, LANES), 1)
    hsl = [slice(h * LANES, (h + 1) * LANES) for h in H]

    def query(h):
        qr = qr_ref[0, :, hsl[h // 2]]
        return jnp.concatenate([qn_ref[0, :, hsl[h]],
                                jnp.where((lane // QK_ROPE) == h % 2, qr, jnp.zeros_like(qr))], axis=1)

    for h in H:
        qt_sc[h] = query(h).T

    def scores(j, h):
        rows = pl.ds(pl.multiple_of(j * tq, tq), tq)
        return _dot(jnp.concatenate([kn_ref[0, rows, hsl[h]], kr_ref[0, rows, :]], axis=1), qt_sc[h])

    def update(carry, s, smax, vt):
        m, acc = carry
        mn = jnp.maximum(m, smax)
        p = jnp.exp2(s - mn)
        vta = jnp.concatenate([vt, jnp.ones((16, vt.shape[1]), BF16)], axis=0)
        return [mn, jnp.exp2(m - mn) * acc + _dot(vta, p)]

    colmax = lambda s: jnp.max(s, axis=0, keepdims=True)

    bmax = []
    for h in H:
        s0 = scores(0, h)
        s_sc[h] = s0
        bmax.append(colmax(s0))

    def step(j, carry):
        rows = pl.ds(pl.multiple_of(j * tq, tq), tq)
        out = []
        for h in H:
            nxt = scores(j + 1, h)
            out += update(carry[3 * h:3 * h + 2], s_sc[h], carry[3 * h + 2], vt_ref[0, hsl[h], rows])
            s_sc[h] = nxt
            out.append(colmax(nxt))
        return out

    row1 = lambda val: jnp.full((1, tq), val, F32)
    carry = []
    for h in H:
        carry += [row1(NEG), jnp.zeros((LANES + 16, tq), F32), bmax[h]]
    carry = lax.fori_loop(0, qi, step, carry)

    krow = lax.broadcasted_iota(jnp.int32, (tq, tq), 0)
    qcol = lax.broadcasted_iota(jnp.int32, (tq, tq), 1)
    causal = krow <= qcol
    drows = pl.ds(pl.multiple_of(qi * tq, tq), tq)
    sm = [_dot(jnp.concatenate([knm_ref[0, :, hsl[h]], krm_ref[0]], axis=1), qt_sc[h])[:nmeta] for h in H]
    pad = jnp.zeros((mp - nmeta, tq), F32)
    for h in H:
        sd = jnp.where(causal, s_sc[h], NEG)
        m, acc = carry[3 * h:3 * h + 2]
        mn = jnp.maximum(m, jnp.maximum(colmax(sm[h]), colmax(sd)))
        p = jnp.concatenate([jnp.exp2(sm[h] - mn), pad, jnp.exp2(sd - mn)], axis=0)
        vta = jnp.concatenate([jnp.concatenate([vtm_ref[0, hsl[h], :], vt_ref[0, hsl[h], drows]], axis=1),
                               jnp.ones((16, mp + tq), BF16)], axis=0)
        acc = jnp.exp2(m - mn) * acc + _dot(vta, p)
        o_ref[0, :, hsl[h]] = (acc[:LANES] / acc[LANES:LANES + 1]).T.astype(o_ref.dtype)


def _attention(qn, qr, kn, kr, vt, knm, krm, vtm, *, batch, seq, tq, nmeta, nh):
    di = qn.shape[1]
    hw = nh * LANES
    r3 = lambda a: a.reshape(batch, seq, a.shape[1])
    mp = knm.shape[0]
    kern = functools.partial(_attn_kernel, tq=tq, nmeta=nmeta)
    return pl.pallas_call(
        kern,
        grid=(batch, di // hw, seq // tq),
        in_specs=[
            pl.BlockSpec((1, tq, hw), lambda b, p, i: (b, i, p)),
            pl.BlockSpec((1, tq, hw // 2), lambda b, p, i: (b, i, p)),
            pl.BlockSpec((1, seq, hw), lambda b, p, i: (b, 0, p)),
            pl.BlockSpec((1, seq, LANES), lambda b, p, i: (b, 0, 0)),
            pl.BlockSpec((1, hw, seq), lambda b, p, i: (0, p, b)),
            pl.BlockSpec((1, mp, hw), lambda b, p, i: (0, 0, p)),
            pl.BlockSpec((1, mp, LANES), lambda b, p, i: (0, 0, 0)),
            pl.BlockSpec((1, hw, mp), lambda b, p, i: (0, p, 0)),
        ],
        out_specs=pl.BlockSpec((1, tq, hw), lambda b, p, i: (b, i, p)),
        out_shape=jax.ShapeDtypeStruct((batch, seq, di), BF16),
        scratch_shapes=[pltpu.VMEM((nh, tq, tq), F32), pltpu.VMEM((nh, 2 * LANES, tq), BF16)],
        compiler_params=_cparams(("arbitrary", "arbitrary", "arbitrary")),
        name="mla_attention",
    )(r3(qn), r3(qr), r3(kn), r3(kr), vt[None], knm[None], krm[None], vtm[None])


def _pick(n, cands):
    for c in cands:
        if n % c == 0:
            return c
    return n


def kernel(x, meta_tokens, norm_pre, norm_post, rwkv_mu, rwkv_w_in, rwkv_w0, rwkv_w2, rwkv_a0, rwkv_a2,
           rwkv_k_k, rwkv_k_a, rwkv_r_k, rwkv_ln_w, rwkv_ln_b, rwkv_w_out,
           mla_w_in, mla_q_norm, mla_w_q_up, mla_kv_norm, mla_w_kv_up, mla_w_out):
    B, T, D = x.shape
    DI = rwkv_w_out.shape[1]
    NP = DI // LANES
    MH = DI // V_HEAD
    nq = mla_q_norm.shape[1]
    nkv = mla_kv_norm.shape[1]
    MP = CHUNK
    assert meta_tokens.shape[0] == N_META and T % CHUNK == 0

    w_in = rwkv_w_in[0]
    w_main = w_in[:, :4 * DI].astype(BF16)
    w_lora = w_in[:, 4 * DI:].astype(BF16)
    mu_main = rwkv_mu[0, :4].reshape(4, 1, D)
    mu_lora = rwkv_mu[0, 4:].reshape(2, 1, D)
    zeros1 = jnp.zeros((DI,), F32)
    par = jnp.stack([rwkv_w0[0], rwkv_a0[0], rwkv_k_k[0], rwkv_k_a[0], rwkv_r_k[0],
                     rwkv_ln_w[0], rwkv_ln_b[0], zeros1])
    zblk = jnp.zeros_like(rwkv_w2[0])
    w2a = jnp.concatenate([jnp.concatenate([rwkv_w2[0], zblk], axis=1),
                           jnp.concatenate([zblk, rwkv_a2[0]], axis=1)], axis=0).astype(BF16)
    w_out0 = rwkv_w_out[0].astype(BF16)

    mw = mla_w_in[0]
    kr_w = mw[:, nq + nkv:nq + nkv + QK_ROPE]
    half = QK_ROPE // 2
    rot = lambda wr: jnp.concatenate([-wr[..., half:], wr[..., :half]], axis=-1)
    win2 = jnp.concatenate([mw[:, :nq + nkv], kr_w, rot(kr_w), mw[:, nq + nkv + QK_ROPE:]],
                           axis=1).astype(BF16)
    wq = mla_w_q_up[0].reshape(nq, MH, QK_NOPE + QK_ROPE)
    wq_r = wq[:, :, QK_NOPE:]
    wq2 = jnp.concatenate([wq[:, :, :QK_NOPE].reshape(nq, MH * QK_NOPE),
                           wq_r.reshape(nq, MH * QK_ROPE),
                           rot(wq_r).reshape(nq, MH * QK_ROPE)], axis=1).astype(BF16)
    wkv = mla_w_kv_up[0].reshape(nkv, MH, QK_NOPE + V_HEAD)
    wkn = wkv[:, :, :QK_NOPE].reshape(nkv, MH * QK_NOPE).astype(BF16)
    wvt = wkv[:, :, QK_NOPE:].reshape(nkv, MH * V_HEAD).T.astype(BF16)
    w_out1 = mla_w_out[0].astype(BF16)

    pos = jnp.arange(N_META + T, dtype=F32)
    inv_freq = jnp.exp(-math.log(ROPE_THETA) * jnp.arange(half, dtype=F32) / half)
    ang = pos[:, None] * inv_freq[None, :]
    rc = jnp.tile(jnp.cos(ang), (1, 4))
    rs = jnp.tile(jnp.sin(ang), (1, 4))
    scale = (QK_NOPE + QK_ROPE) ** -0.5 * math.log2(math.e)

    gpre0 = norm_pre[0:1]
    gpre1 = norm_pre[1:2]
    gpost0 = norm_post[0:1]
    gpost1 = norm_post[1:2]

    xm = jnp.concatenate([jnp.zeros((MP - N_META, D), x.dtype), meta_tokens.astype(x.dtype)], axis=0)
    zero8 = jnp.zeros((8, D), x.dtype)
    hp_m, hl_m = _inproj(xm, zero8, gpre0, mu_main, mu_lora, w_main, w_lora, seq=MP, tr=MP)
    s_zero = jnp.zeros((NP, HEAD, LANES), F32)
    y_m, s_meta = _wkv(hp_m, hl_m, par, w2a, s_zero, batch=1, seq=MP, tb=MP)
    h1_m = _outproj(y_m.reshape(MP, DI), hp_m, 3, xm, w_out0, gpost0, tr=MP)
    rc_m = jnp.concatenate([jnp.zeros((MP - N_META, LANES), F32), rc[:N_META]], axis=0)
    rs_m = jnp.concatenate([jnp.zeros((MP - N_META, LANES), F32), rs[:N_META]], axis=0)
    _, _, kn_m, vt_m, kr_m, _ = _mlaproj(h1_m, gpre1, win2, mla_q_norm, wq2, mla_kv_norm, wkn, wvt,
                                         rc_m, rs_m, tr=MP, scale=scale)
    padrows = lambda a: jnp.concatenate(
        [a[MP - N_META:], jnp.zeros((LANES - N_META, a.shape[1]), a.dtype)], axis=0)
    knm, krm = padrows(kn_m), padrows(kr_m)
    vtm = jnp.concatenate([vt_m[:, MP - N_META:], jnp.zeros((DI, LANES - N_META), vt_m.dtype)], axis=1)

    x2 = x.reshape(B * T, D)
    tr_in = _pick(T, (1024, 512, 256, 128, 64))
    hp, hl = _inproj(x2, meta_tokens[N_META - 8:].astype(x.dtype), gpre0, mu_main, mu_lora,
                     w_main, w_lora, seq=T, tr=tr_in)
    y, _ = _wkv(hp, hl, par, w2a, s_meta[0], batch=B, seq=T, tb=_pick(T, (4 * CHUNK, 2 * CHUNK, CHUNK)))
    tr_o = _pick(T, (1024, 512, 256, 128, 64))
    h1 = _outproj(y.reshape(B * T, DI), hp, 3, x2, w_out0, gpost0, tr=tr_o)

    tr_p = _pick(T, (512, 256, 128, 64))
    qn, qr, kn, vt, kr, g1 = _mlaproj(h1, gpre1, win2, mla_q_norm, wq2, mla_kv_norm, wkn, wvt,
                                      rc[N_META:], rs[N_META:], tr=tr_p, scale=scale)
    tq = _pick(T, (512, 256, 128))
    o = _attention(qn, qr, kn, kr, vt, knm, krm, vtm, batch=B, seq=T, tq=tq, nmeta=N_META, nh=4)
    out = _outproj(o.reshape(B * T, DI), g1, 0, h1, w_out1, gpost1, tr=tr_o)
    return out.reshape(B, T, D)
```

```python
import functools
import math

import jax
import jax.numpy as jnp
from jax import lax
from jax.experimental import pallas as pl
from jax.experimental.pallas import tpu as pltpu

F32 = jnp.float32
BF16 = jnp.bfloat16

RMS_EPS = 1e-6
GN_EPS = 64e-5
ROPE_THETA = 10000.0
N_META = 16
HEAD = 64
LANES = 128
CHUNK = 64
QK_NOPE = 128
QK_ROPE = 64
V_HEAD = 128
VMEM_LIMIT = 56 * 1024 * 1024


def _cparams(sem):
    return pltpu.CompilerParams(dimension_semantics=sem, vmem_limit_bytes=VMEM_LIMIT)


def _rms(x, g):
    return x * lax.rsqrt(jnp.mean(x * x, axis=-1, keepdims=True) + RMS_EPS) * g


def _dot(a, b):
    return jnp.dot(a.astype(BF16), b.astype(BF16), preferred_element_type=F32)


def _dot_nt(a, b):
    return lax.dot_general(a.astype(BF16), b.astype(BF16), (((1,), (1,)), ((), ())),
                           preferred_element_type=F32)


def _dot_tn(a, b):
    return lax.dot_general(a.astype(BF16), b.astype(BF16), (((0,), (0,)), ((), ())),
                           preferred_element_type=F32)


def _inproj_kernel(x_ref, xprev_ref, x0_ref, g_ref, mu_ref, mul_ref, w_ref, wl_ref,
                   out_ref, outl_ref, u_sc, dx_sc, *, tiles_per_seq):
    i = pl.program_id(0)
    j = pl.program_id(1)

    @pl.when(j == 0)
    def _():
        g = g_ref[...]
        u = _rms(x_ref[...], g)
        first = (i % tiles_per_seq) == 0
        prow = jnp.where(first, x0_ref[7:8, :], xprev_ref[7:8, :])
        up = _rms(prow, g)
        rows = lax.broadcasted_iota(jnp.int32, u.shape, 0)
        ush = jnp.where(rows == 0, up, pltpu.roll(u, 1, 0))
        dx = ush - u
        u_sc[...] = u
        dx_sc[...] = dx
        wl = wl_ref[...]
        ow = _dot(u + mul_ref[0] * dx, wl)
        oa = _dot(u + mul_ref[1] * dx, wl)
        lane = lax.broadcasted_iota(jnp.int32, ow.shape, 1)
        outl_ref[...] = jnp.where(lane < HEAD, ow, oa).astype(outl_ref.dtype)

    xg = u_sc[...] + mu_ref[0] * dx_sc[...]
    out_ref[...] = _dot(xg, w_ref[...]).astype(out_ref.dtype)


def _inproj(x2, x0, g, mu_main, mu_lora, w_main, w_lora, *, seq, tr):
    rows, d = x2.shape
    ncol = w_main.shape[1]
    ngroups = mu_main.shape[0]
    gw = ncol // ngroups
    tpb = tr // 8
    kern = functools.partial(_inproj_kernel, tiles_per_seq=seq // tr)
    return pl.pallas_call(
        kern,
        grid=(rows // tr, ngroups),
        in_specs=[
            pl.BlockSpec((tr, d), lambda i, j: (i, 0)),
            pl.BlockSpec((8, d), lambda i, j: (jnp.maximum(i * tpb - 1, 0), 0)),
            pl.BlockSpec((8, d), lambda i, j: (0, 0)),
            pl.BlockSpec((1, d), lambda i, j: (0, 0)),
            pl.BlockSpec((1, 1, d), lambda i, j: (j, 0, 0)),
            pl.BlockSpec((2, 1, d), lambda i, j: (0, 0, 0)),
            pl.BlockSpec((d, gw), lambda i, j: (0, j)),
            pl.BlockSpec((d, LANES), lambda i, j: (0, 0)),
        ],
        out_specs=[
            pl.BlockSpec((tr, gw), lambda i, j: (i, j)),
            pl.BlockSpec((tr, LANES), lambda i, j: (i, 0)),
        ],
        out_shape=[
            jax.ShapeDtypeStruct((rows, ncol), BF16),
            jax.ShapeDtypeStruct((rows, LANES), BF16),
        ],
        scratch_shapes=[pltpu.VMEM((tr, d), F32), pltpu.VMEM((tr, d), F32)],
        compiler_params=_cparams(("arbitrary", "arbitrary")),
        name="rwkv_inproj",
    )(x2, x2, x0, g, mu_main, mu_lora, w_main, w_lora)


def _bd(x, m0):
    zero = jnp.zeros_like(x)
    return jnp.concatenate([jnp.where(m0, x, zero), jnp.where(m0, zero, x)], axis=0)


def _wkv_kernel(r_ref, k_ref, v_ref, lo_ref, par_ref, w2a_ref, s0_ref, y_ref, sout_ref, s_sc, *, npairs):
    C = CHUNK
    di = npairs * LANES
    nch = r_ref.shape[1] // C

    @pl.when(pl.program_id(1) == 0)
    def _():
        s_sc[...] = s0_ref[...]

    row2 = lax.broadcasted_iota(jnp.int32, (LANES, LANES), 0)
    lane2 = lax.broadcasted_iota(jnp.int32, (LANES, LANES), 1)
    ones_bd = jnp.where((row2 // HEAD) == (lane2 // HEAD), 1.0, 0.0).astype(BF16)
    P = range(npairs)

    tri = jnp.where(lax.broadcasted_iota(jnp.int32, (C, C), 0)
                    >= lax.broadcasted_iota(jnp.int32, (C, C), 1), 1.0, 0.0).astype(BF16)
    tpos = lax.broadcasted_iota(jnp.int32, (C, LANES), 0)
    lane_c = lax.broadcasted_iota(jnp.int32, (C, LANES), 1)
    spos = lane_c % HEAD
    m0 = lane_c < HEAD
    strict = tpos > spos
    incl = tpos >= spos
    eye = jnp.where(tpos == spos, 1.0, 0.0).astype(F32)
    cat0 = lambda x, y: jnp.concatenate([x, y], axis=0)
    cat1 = lambda x, y: jnp.concatenate([x, y], axis=1)
    b16 = lambda x: x.astype(BF16)

    def head_sum_mxu(xs):
        st = _dot(jnp.concatenate(xs, axis=0), ones_bd)
        return [st[i * C:(i + 1) * C] for i in range(len(xs))]

    def head_sum(xs):
        out = []
        for x in xs:
            s0 = jnp.sum(jnp.where(m0, x, 0.0), axis=1, keepdims=True)
            s1 = jnp.sum(jnp.where(m0, 0.0, x), axis=1, keepdims=True)
            out.append(jnp.where(m0, s0, s1))
        return out

    lo = lo_ref[0].astype(F32)
    tl = b16(jnp.where(lax.broadcasted_iota(jnp.int32, lo.shape, 1) < HEAD, jnp.tanh(lo), lo))
    wa_all = [_dot(tl, cat1(w2a_ref[:, p * LANES:(p + 1) * LANES], w2a_ref[:, di + p * LANES:di + (p + 1) * LANES]))
              for p in P]

    def prep(c, pairs, out):
        rows = slice(c * C, (c + 1) * C)
        psl = {p: slice(p * LANES, (p + 1) * LANES) for p in pairs}
        par = lambda n, p: par_ref[n:n + 1, psl[p]]
        wa = {p: wa_all[p][rows] for p in pairs}
        k = {p: k_ref[0, rows, psl[p]].astype(F32) for p in pairs}
        kk = {p: k[p] * par(2, p) for p in pairs}
        n2 = dict(zip(pairs, head_sum([kk[p] * kk[p] for p in pairs])))
        yield
        lw = {p: -math.exp(-0.5) * jax.nn.sigmoid(par(0, p) + wa[p][:, :LANES]) for p in pairs}
        a = {p: jax.nn.sigmoid(par(1, p) + wa[p][:, LANES:]) for p in pairs}
        hi = {p: b16(lw[p]) for p in pairs}
        cs2 = {p: jnp.dot(tri, cat1(hi[p], b16(lw[p] - hi[p].astype(F32))), preferred_element_type=F32)
               for p in pairs}
        yield
        r = {p: r_ref[0, rows, psl[p]].astype(F32) for p in pairs}
        v = {p: v_ref[0, rows, psl[p]].astype(F32) for p in pairs}
        km = {p: k[p] * (1.0 + (a[p] - 1.0) * par(3, p)) for p in pairs}
        bsum = dict(zip(pairs, head_sum_mxu([r[p] * km[p] * par(4, p) for p in pairs])))
        yield
        for p in pairs:
            kkn = kk[p] * lax.rsqrt(jnp.maximum(n2[p], 1e-24))
            bv = kkn * a[p]
            cs = cs2[p][:, :LANES] + cs2[p][:, LANES:]
            last = cs[C - 1:C]
            e_out = jnp.exp(-cs)
            e_l = jnp.exp(last - cs)
            rt = r[p] * jnp.exp(cs)
            out[p] = dict(
                wc=jnp.exp(last), bonus=bsum[p] * v[p],
                vb=b16(v[p]), rtb=b16(rt), atb=b16(-kkn * jnp.exp(cs - lw[p])),
                kpb=b16(km[p] * e_l), bpb=b16(bv * e_l),
                zt=cat1(_bd(b16(bv * e_out), m0).T, _bd(b16(km[p] * e_out), m0).T))
        yield

    def staged(fn, out):
        for p in P:
            out.append(fn(p))
            if p % 4 == 3:
                yield

    def indep(d, res):
        sc = []
        yield from staged(lambda p: _dot(cat0(d[p]["atb"], d[p]["rtb"]), d[p]["zt"]), sc)
        ab = [jnp.where(strict, sc[p][:C, :LANES], 0.0) for p in P]
        akrk = [b16(cat0(jnp.where(strict, sc[p][:C, LANES:], 0.0), jnp.where(incl, sc[p][C:, LANES:], 0.0)))
                for p in P]
        res["rbb"] = [b16(jnp.where(incl, sc[p][C:, :LANES], 0.0)) for p in P]

        qb = [b16(ab[p]) for p in P]
        pm = [eye + ab[p] for p in P]
        q = []
        yield from staged(lambda p: _dot(qb[p], _bd(qb[p], m0)), q)
        for _ in range(4):
            qb = [b16(q[p]) for p in P]
            pq = []
            yield from staged(lambda p: _dot(cat0(b16(pm[p]), qb[p]), _bd(qb[p], m0)), pq)
            pm = [pm[p] + pq[p][:C] for p in P]
            q = [pq[p][C:] for p in P]
        res["pm"] = []
        yield from staged(lambda p: b16(pm[p] + _dot(pm[p], _bd(b16(q[p]), m0))), res["pm"])
        res["lv"] = []
        yield from staged(lambda p: _dot(akrk[p], _bd(d[p]["vb"], m0)), res["lv"])

    def dep(c, d, res, state):
        pm, lv, rbb = res["pm"], res["lv"], res["rbb"]
        ars = []
        def sbt(p):
            t = _bd(b16(state[p]), m0).T
            return cat0(t[:HEAD], t[HEAD:])
        yield from staged(lambda p: _dot(cat0(d[p]["atb"], d[p]["rtb"]), sbt(p)), ars)
        ub = []
        yield from staged(lambda p: b16(_dot(pm[p], _bd(b16(ars[p][:C] + lv[p][:C]), m0))), ub)
        ys = []
        yield from staged(lambda p: ars[p][C:] + lv[p][C:] + _dot(rbb[p], _bd(ub[p], m0)), ys)
        sfull = []
        yield from staged(lambda p: _dot(cat0(ub[p], d[p]["vb"]).T, cat0(d[p]["bpb"], d[p]["kpb"])), sfull)
        for p in P:
            state[p] = state[p] * d[p]["wc"] + jnp.where(m0, sfull[p][:HEAD], sfull[p][HEAD:])
        mean = head_sum(ys)
        dd = [ys[p] - mean[p] * (1.0 / HEAD) for p in P]
        yield
        var = [v_ * (1.0 / HEAD) for v_ in head_sum([dd[p] * dd[p] for p in P])]
        yield
        for p in P:
            ps = slice(p * LANES, (p + 1) * LANES)
            yn = dd[p] * lax.rsqrt(var[p] + GN_EPS) * par_ref[5:6, ps] + par_ref[6:7, ps] + d[p]["bonus"]
            y_ref[0, c * C:(c + 1) * C, ps] = yn.astype(y_ref.dtype)
            if p % 4 == 3:
                yield

    def prep_all(c, out):
        half = npairs // 2
        yield from prep(c, range(half), out)
        yield from prep(c, range(half, npairs), out)

    def interleave(gens):
        gens = list(gens)
        while gens:
            for g in list(gens):
                if next(g, StopIteration) is StopIteration:
                    gens.remove(g)

    state = [s_sc[p] for p in P]
    prepd = [[None] * npairs for _ in range(nch)]
    res = [dict() for _ in range(nch)]
    interleave([prep_all(0, prepd[0])])
    for ph in range(nch + 1):
        gens = []
        if ph >= 1:
            gens.append(dep(ph - 1, prepd[ph - 1], res[ph - 1], state))
        if ph < nch:
            gens.append(indep(prepd[ph], res[ph]))
        if ph + 1 < nch:
            gens.append(prep_all(ph + 1, prepd[ph + 1]))
        interleave(gens)
    for p in P:
        s_sc[p] = state[p]
        sout_ref[0, p] = state[p]


def _wkv(hp, hl, par, w2a, s0, *, batch, seq, tb):
    d_inner = hp.shape[1] // 4
    npairs = d_inner // LANES
    hp3 = hp.reshape(batch, seq, 4 * d_inner)
    hl3 = hl.reshape(batch, seq, LANES)
    kern = functools.partial(_wkv_kernel, npairs=npairs)
    return pl.pallas_call(
        kern,
        grid=(batch, seq // tb),
        in_specs=[
            pl.BlockSpec((1, tb, d_inner), lambda b, t: (b, t, 0)),
            pl.BlockSpec((1, tb, d_inner), lambda b, t: (b, t, 1)),
            pl.BlockSpec((1, tb, d_inner), lambda b, t: (b, t, 2)),
            pl.BlockSpec((1, tb, LANES), lambda b, t: (b, t, 0)),
            pl.BlockSpec((8, d_inner), lambda b, t: (0, 0)),
            pl.BlockSpec((LANES, 2 * d_inner), lambda b, t: (0, 0)),
            pl.BlockSpec((npairs, HEAD, LANES), lambda b, t: (0, 0, 0)),
        ],
        out_specs=[
            pl.BlockSpec((1, tb, d_inner), lambda b, t: (b, t, 0)),
            pl.BlockSpec((1, npairs, HEAD, LANES), lambda b, t: (b, 0, 0, 0)),
        ],
        out_shape=[
            jax.ShapeDtypeStruct((batch, seq, d_inner), BF16),
            jax.ShapeDtypeStruct((batch, npairs, HEAD, LANES), F32),
        ],
        scratch_shapes=[pltpu.VMEM((npairs, HEAD, LANES), F32)],
        compiler_params=_cparams(("arbitrary", "arbitrary")),
        name="wkv7",
    )(hp3, hp3, hp3, hl3, par, w2a, s0)


def _outproj_kernel(y_ref, g_ref, res_ref, w_ref, gn_ref, out_ref):
    g = g_ref[...].astype(F32)
    z = y_ref[...].astype(F32) * (g * jax.nn.sigmoid(g))
    m = _dot(z, w_ref[...])
    out_ref[...] = res_ref[...] + _rms(m, gn_ref[...])


def _outproj(y2, gsrc, gcol, res2, w, gn, *, tr):
    rows, di = y2.shape
    d = w.shape[1]
    return pl.pallas_call(
        _outproj_kernel,
        grid=(rows // tr,),
        in_specs=[
            pl.BlockSpec((tr, di), lambda i: (i, 0)),
            pl.BlockSpec((tr, di), lambda i: (i, gcol)),
            pl.BlockSpec((tr, d), lambda i: (i, 0)),
            pl.BlockSpec((di, d), lambda i: (0, 0)),
            pl.BlockSpec((1, d), lambda i: (0, 0)),
        ],
        out_specs=pl.BlockSpec((tr, d), lambda i: (i, 0)),
        out_shape=jax.ShapeDtypeStruct((rows, d), F32),
        compiler_params=_cparams(("arbitrary",)),
        name="gate_outproj",
    )(y2, gsrc, res2, w, gn)


def _mlaproj_kernel(h_ref, gpre_ref, win_ref, qn_ref, wq_ref, kvn_ref, wkn_ref, wvt_ref, rc_ref, rs_ref,
                    oqn_ref, oqr_ref, okn_ref, ovt_ref, okr_ref, og_ref, *, scale):
    u = _rms(h_ref[...], gpre_ref[...])
    t = _dot(u, win_ref[...])
    nq = qn_ref.shape[1]
    nkv = kvn_ref.shape[1]
    di = og_ref.shape[1]
    c_q = t[:, :nq]
    c_kv = t[:, nq:nq + nkv]
    krr = t[:, nq + nkv:nq + nkv + LANES]
    og_ref[...] = t[:, nq + nkv + LANES:].astype(og_ref.dtype)

    rc = rc_ref[...]
    rs = rs_ref[...]
    lane = lax.broadcasted_iota(jnp.int32, krr.shape, 1)
    prod = krr * jnp.where(lane < QK_ROPE, rc, rs)
    okr_ref[...] = (prod + pltpu.roll(prod, QK_ROPE, 1)).astype(okr_ref.dtype)

    q = _dot(_rms(c_q, qn_ref[...]), wq_ref[...]) * scale
    oqn_ref[...] = q[:, :di].astype(oqn_ref.dtype)
    nrep = (di // 2) // LANES
    rct = jnp.concatenate([rc] * nrep, axis=1)
    rst = jnp.concatenate([rs] * nrep, axis=1)
    oqr_ref[...] = (q[:, di:di + di // 2] * rct + q[:, di + di // 2:] * rst).astype(oqr_ref.dtype)

    ckv = _rms(c_kv, kvn_ref[...])
    okn_ref[...] = _dot(ckv, wkn_ref[...]).astype(okn_ref.dtype)
    ovt_ref[...] = _dot_nt(wvt_ref[...], ckv).astype(ovt_ref.dtype)


def _mlaproj(h2, gpre, win, qn, wq, kvn, wkn, wvt, rc, rs, *, tr, scale):
    rows, d = h2.shape
    di = wkn.shape[1]
    full = lambda a: pl.BlockSpec(a.shape, lambda i: (0, 0))
    rowblk = lambda n: pl.BlockSpec((tr, n), lambda i: (i, 0))
    tps = rc.shape[0] // tr
    rope = pl.BlockSpec((tr, LANES), lambda i: (i % tps, 0))
    kern = functools.partial(_mlaproj_kernel, scale=scale)
    return pl.pallas_call(
        kern,
        grid=(rows // tr,),
        in_specs=[rowblk(d), full(gpre), full(win), full(qn), full(wq), full(kvn), full(wkn), full(wvt),
                  rope, rope],
        out_specs=[rowblk(di), rowblk(di // 2), rowblk(di), pl.BlockSpec((di, tr), lambda i: (0, i)),
                   rowblk(LANES), rowblk(di)],
        out_shape=[
            jax.ShapeDtypeStruct((rows, di), BF16),
            jax.ShapeDtypeStruct((rows, di // 2), BF16),
            jax.ShapeDtypeStruct((rows, di), BF16),
            jax.ShapeDtypeStruct((di, rows), BF16),
            jax.ShapeDtypeStruct((rows, LANES), BF16),
            jax.ShapeDtypeStruct((rows, di), BF16),
        ],
        compiler_params=_cparams(("arbitrary",)),
        name="mla_proj",
    )(h2, gpre, win, qn, wq, kvn, wkn, wvt, rc, rs)


NEG = -1e30


def _attn_kernel(qn_ref, qr_ref, kn_ref, kr_ref, vt_ref, knm_ref, krm_ref, vtm_ref, o_ref, s_sc, qt_sc, *,
                 tq, nmeta):
    qi = pl.program_id(2)
    mp = knm_ref.shape[1]
    nh = qn_ref.shape[2] // LANES
    H = range(nh)
    lane = lax.broadcasted_iota(jnp.int32, (tq, LANES), 1)
    hsl = [slice(h * LANES, (h + 1) * LANES) for h in H]

    def query(h):
        qr = qr_ref[0, :, hsl[h // 2]]
        return jnp.concatenate([qn_ref[0, :, hsl[h]],
                                jnp.where((lane // QK_ROPE) == h % 2, qr, jnp.zeros_like(qr))], axis=1)

    for h in H:
        qt_sc[h] = query(h).T

    def scores(j, h):
        rows = pl.ds(pl.multiple_of(j * tq, tq), tq)
        return _dot(jnp.concatenate([kn_ref[0, rows, hsl[h]], kr_ref[0, rows, :]], axis=1), qt_sc[h])

    def update(carry, s, smax, vt):
        m, acc = carry
        mn = jnp.maximum(m, smax)
        p = jnp.exp2(s - mn)
        vta = jnp.concatenate([vt, jnp.ones((16, vt.shape[1]), BF16)], axis=0)
        return [mn, jnp.exp2(m - mn) * acc + _dot(vta, p)]

    colmax = lambda s: jnp.max(s, axis=0, keepdims=True)

    bmax = []
    for h in H:
        s0 = scores(0, h)
        s_sc[h] = s0
        bmax.append(colmax(s0))

    def step(j, carry):
        rows = pl.ds(pl.multiple_of(j * tq, tq), tq)
        out = []
        for h in H:
            nxt = scores(j + 1, h)
            out += update(carry[3 * h:3 * h + 2], s_sc[h], carry[3 * h + 2], vt_ref[0, hsl[h], rows])
            s_sc[h] = nxt
            out.append(colmax(nxt))
        return out

    row1 = lambda val: jnp.full((1, tq), val, F32)
    carry = []
    for h in H:
        carry += [row1(NEG), jnp.zeros((LANES + 16, tq), F32), bmax[h]]
    carry = lax.fori_loop(0, qi, step, carry)

    krow = lax.broadcasted_iota(jnp.int32, (tq, tq), 0)
    qcol = lax.broadcasted_iota(jnp.int32, (tq, tq), 1)
    causal = krow <= qcol
    drows = pl.ds(pl.multiple_of(qi * tq, tq), tq)
    sm = [_dot(jnp.concatenate([knm_ref[0, :, hsl[h]], krm_ref[0]], axis=1), qt_sc[h])[:nmeta] for h in H]
    pad = jnp.zeros((mp - nmeta, tq), F32)
    for h in H:
        sd = jnp.where(causal, s_sc[h], NEG)
        m, acc = carry[3 * h:3 * h + 2]
        mn = jnp.maximum(m, jnp.maximum(colmax(sm[h]), colmax(sd)))
        p = jnp.concatenate([jnp.exp2(sm[h] - mn), pad, jnp.exp2(sd - mn)], axis=0)
        vta = jnp.concatenate([jnp.concatenate([vtm_ref[0, hsl[h], :], vt_ref[0, hsl[h], drows]], axis=1),
                               jnp.ones((16, mp + tq), BF16)], axis=0)
        acc = jnp.exp2(m - mn) * acc + _dot(vta, p)
        o_ref[0, :, hsl[h]] = (acc[:LANES] / acc[LANES:LANES + 1]).T.astype(o_ref.dtype)


def _attention(qn, qr, kn, kr, vt, knm, krm, vtm, *, batch, seq, tq, nmeta, nh):
    di = qn.shape[1]
    hw = nh * LANES
    r3 = lambda a: a.reshape(batch, seq, a.shape[1])
    mp = knm.shape[0]
    kern = functools.partial(_attn_kernel, tq=tq, nmeta=nmeta)
    return pl.pallas_call(
        kern,
        grid=(batch, di // hw, seq // tq),
        in_specs=[
            pl.BlockSpec((1, tq, hw), lambda b, p, i: (b, i, p)),
            pl.BlockSpec((1, tq, hw // 2), lambda b, p, i: (b, i, p)),
            pl.BlockSpec((1, seq, hw), lambda b, p, i: (b, 0, p)),
            pl.BlockSpec((1, seq, LANES), lambda b, p, i: (b, 0, 0)),
            pl.BlockSpec((1, hw, seq), lambda b, p, i: (0, p, b)),
            pl.BlockSpec((1, mp, hw), lambda b, p, i: (0, 0, p)),
            pl.BlockSpec((1, mp, LANES), lambda b, p, i: (0, 0, 0)),
            pl.BlockSpec((1, hw, mp), lambda b, p, i: (0, p, 0)),
        ],
        out_specs=pl.BlockSpec((1, tq, hw), lambda b, p, i: (b, i, p)),
        out_shape=jax.ShapeDtypeStruct((batch, seq, di), BF16),
        scratch_shapes=[pltpu.VMEM((nh, tq, tq), F32), pltpu.VMEM((nh, 2 * LANES, tq), BF16)],
        compiler_params=_cparams(("arbitrary", "arbitrary", "arbitrary")),
        name="mla_attention",
    )(r3(qn), r3(qr), r3(kn), r3(kr), vt[None], knm[None], krm[None], vtm[None])


def _pick(n, cands):
    for c in cands:
        if n % c == 0:
            return c
    return n


def kernel(x, meta_tokens, norm_pre, norm_post, rwkv_mu, rwkv_w_in, rwkv_w0, rwkv_w2, rwkv_a0, rwkv_a2,
           rwkv_k_k, rwkv_k_a, rwkv_r_k, rwkv_ln_w, rwkv_ln_b, rwkv_w_out,
           mla_w_in, mla_q_norm, mla_w_q_up, mla_kv_norm, mla_w_kv_up, mla_w_out):
    B, T, D = x.shape
    DI = rwkv_w_out.shape[1]
    NP = DI // LANES
    MH = DI // V_HEAD
    nq = mla_q_norm.shape[1]
    nkv = mla_kv_norm.shape[1]
    MP = CHUNK
    assert meta_tokens.shape[0] == N_META and T % CHUNK == 0

    w_in = rwkv_w_in[0]
    w_main = w_in[:, :4 * DI].astype(BF16)
    w_lora = w_in[:, 4 * DI:].astype(BF16)
    mu_main = rwkv_mu[0, :4].reshape(4, 1, D)
    mu_lora = rwkv_mu[0, 4:].reshape(2, 1, D)
    zeros1 = jnp.zeros((DI,), F32)
    par = jnp.stack([rwkv_w0[0], rwkv_a0[0], rwkv_k_k[0], rwkv_k_a[0], rwkv_r_k[0],
                     rwkv_ln_w[0], rwkv_ln_b[0], zeros1])
    zblk = jnp.zeros_like(rwkv_w2[0])
    w2a = jnp.concatenate([jnp.concatenate([rwkv_w2[0], zblk], axis=1),
                           jnp.concatenate([zblk, rwkv_a2[0]], axis=1)], axis=0).astype(BF16)
    w_out0 = rwkv_w_out[0].astype(BF16)

    mw = mla_w_in[0]
    kr_w = mw[:, nq + nkv:nq + nkv + QK_ROPE]
    half = QK_ROPE // 2
    rot = lambda wr: jnp.concatenate([-wr[..., half:], wr[..., :half]], axis=-1)
    win2 = jnp.concatenate([mw[:, :nq + nkv], kr_w, rot(kr_w), mw[:, nq + nkv + QK_ROPE:]],
                           axis=1).astype(BF16)
    wq = mla_w_q_up[0].reshape(nq, MH, QK_NOPE + QK_ROPE)
    wq_r = wq[:, :, QK_NOPE:]
    wq2 = jnp.concatenate([wq[:, :, :QK_NOPE].reshape(nq, MH * QK_NOPE),
                           wq_r.reshape(nq, MH * QK_ROPE),
                           rot(wq_r).reshape(nq, MH * QK_ROPE)], axis=1).astype(BF16)
    wkv = mla_w_kv_up[0].reshape(nkv, MH, QK_NOPE + V_HEAD)
    wkn = wkv[:, :, :QK_NOPE].reshape(nkv, MH * QK_NOPE).astype(BF16)
    wvt = wkv[:, :, QK_NOPE:].reshape(nkv, MH * V_HEAD).T.astype(BF16)
    w_out1 = mla_w_out[0].astype(BF16)

    pos = jnp.arange(N_META + T, dtype=F32)
    inv_freq = jnp.exp(-math.log(ROPE_THETA) * jnp.arange(half, dtype=F32) / half)
    ang = pos[:, None] * inv_freq[None, :]
    rc = jnp.tile(jnp.cos(ang), (1, 4))
    rs = jnp.tile(jnp.sin(ang), (1, 4))
    scale = (QK_NOPE + QK_ROPE) ** -0.5 * math.log2(math.e)

    gpre0 = norm_pre[0:1]
    gpre1 = norm_pre[1:2]
    gpost0 = norm_post[0:1]
    gpost1 = norm_post[1:2]

    xm = jnp.concatenate([jnp.zeros((MP - N_META, D), x.dtype), meta_tokens.astype(x.dtype)], axis=0)
    zero8 = jnp.zeros((8, D), x.dtype)
    hp_m, hl_m = _inproj(xm, zero8, gpre0, mu_main, mu_lora, w_main, w_lora, seq=MP, tr=MP)
    s_zero = jnp.zeros((NP, HEAD, LANES), F32)
    y_m, s_meta = _wkv(hp_m, hl_m, par, w2a, s_zero, batch=1, seq=MP, tb=MP)
    h1_m = _outproj(y_m.reshape(MP, DI), hp_m, 3, xm, w_out0, gpost0, tr=MP)
    rc_m = jnp.concatenate([jnp.zeros((MP - N_META, LANES), F32), rc[:N_META]], axis=0)
    rs_m = jnp.concatenate([jnp.zeros((MP - N_META, LANES), F32), rs[:N_META]], axis=0)
    _, _, kn_m, vt_m, kr_m, _ = _mlaproj(h1_m, gpre1, win2, mla_q_norm, wq2, mla_kv_norm, wkn, wvt,
                                         rc_m, rs_m, tr=MP, scale=scale)
    padrows = lambda a: jnp.concatenate(
        [a[MP - N_META:], jnp.zeros((LANES - N_META, a.shape[1]), a.dtype)], axis=0)
    knm, krm = padrows(kn_m), padrows(kr_m)
    vtm = jnp.concatenate([vt_m[:, MP - N_META:], jnp.zeros((DI, LANES - N_META), vt_m.dtype)], axis=1)

    x2 = x.reshape(B * T, D)
    tr_in = _pick(T, (1024, 512, 256, 128, 64))
    hp, hl = _inproj(x2, meta_tokens[N_META - 8:].astype(x.dtype), gpre0, mu_main, mu_lora,
                     w_main, w_lora, seq=T, tr=tr_in)
    y, _ = _wkv(hp, hl, par, w2a, s_meta[0], batch=B, seq=T, tb=_pick(T, (4 * CHUNK, 2 * CHUNK, CHUNK)))
    tr_o = _pick(T, (1024, 512, 256, 128, 64))
    h1 = _outproj(y.reshape(B * T, DI), hp, 3, x2, w_out0, gpost0, tr=tr_o)

    tr_p = _pick(T, (512, 256, 128, 64))
    qn, qr, kn, vt, kr, g1 = _mlaproj(h1, gpre1, win2, mla_q_norm, wq2, mla_kv_norm, wkn, wvt,
                                      rc[N_META:], rs[N_META:], tr=tr_p, scale=scale)
    tq = _pick(T, (512, 256, 128))
    o = _attention(qn, qr, kn, kr, vt, knm, krm, vtm, batch=B, seq=T, tq=tq, nmeta=N_META, nh=4)
    out = _outproj(o.reshape(B * T, DI), g1, 0, h1, w_out1, gpost1, tr=tr_o)
    return out.reshape(B, T, D)
```

```python
import functools
import math

import jax
import jax.numpy as jnp
from jax import lax
from jax.experimental import pallas as pl
from jax.experimental.pallas import tpu as pltpu

F32 = jnp.float32
BF16 = jnp.bfloat16

RMS_EPS = 1e-6
GN_EPS = 64e-5
ROPE_THETA = 10000.0
N_META = 16
HEAD = 64
LANES = 128
CHUNK = 64
QK_NOPE = 128
QK_ROPE = 64
V_HEAD = 128
VMEM_LIMIT = 56 * 1024 * 1024


def _cparams(sem):
    return pltpu.CompilerParams(dimension_semantics=sem, vmem_limit_bytes=VMEM_LIMIT)


def _rms(x, g):
    return x * lax.rsqrt(jnp.mean(x * x, axis=-1, keepdims=True) + RMS_EPS) * g


def _dot(a, b):
    return jnp.dot(a.astype(BF16), b.astype(BF16), preferred_element_type=F32)


def _dot_nt(a, b):
    return lax.dot_general(a.astype(BF16), b.astype(BF16), (((1,), (1,)), ((), ())),
                           preferred_element_type=F32)


def _dot_tn(a, b):
    return lax.dot_general(a.astype(BF16), b.astype(BF16), (((0,), (0,)), ((), ())),
                           preferred_element_type=F32)


def _inproj_kernel(x_ref, xprev_ref, x0_ref, g_ref, mu_ref, mul_ref, w_ref, wl_ref,
                   out_ref, outl_ref, u_sc, dx_sc, *, tiles_per_seq):
    i = pl.program_id(0)
    j = pl.program_id(1)

    @pl.when(j == 0)
    def _():
        g = g_ref[...]
        u = _rms(x_ref[...], g)
        first = (i % tiles_per_seq) == 0
        prow = jnp.where(first, x0_ref[7:8, :], xprev_ref[7:8, :])
        up = _rms(prow, g)
        rows = lax.broadcasted_iota(jnp.int32, u.shape, 0)
        ush = jnp.where(rows == 0, up, pltpu.roll(u, 1, 0))
        dx = ush - u
        u_sc[...] = u
        dx_sc[...] = dx
        wl = wl_ref[...]
        ow = _dot(u + mul_ref[0] * dx, wl)
        oa = _dot(u + mul_ref[1] * dx, wl)
        lane = lax.broadcasted_iota(jnp.int32, ow.shape, 1)
        outl_ref[...] = jnp.where(lane < HEAD, ow, oa).astype(outl_ref.dtype)

    xg = u_sc[...] + mu_ref[0] * dx_sc[...]
    out_ref[...] = _dot(xg, w_ref[...]).astype(out_ref.dtype)


def _inproj(x2, x0, g, mu_main, mu_lora, w_main, w_lora, *, seq, tr):
    rows, d = x2.shape
    ncol = w_main.shape[1]
    ngroups = mu_main.shape[0]
    gw = ncol // ngroups
    tpb = tr // 8
    kern = functools.partial(_inproj_kernel, tiles_per_seq=seq // tr)
    return pl.pallas_call(
        kern,
        grid=(rows // tr, ngroups),
        in_specs=[
            pl.BlockSpec((tr, d), lambda i, j: (i, 0)),
            pl.BlockSpec((8, d), lambda i, j: (jnp.maximum(i * tpb - 1, 0), 0)),
            pl.BlockSpec((8, d), lambda i, j: (0, 0)),
            pl.BlockSpec((1, d), lambda i, j: (0, 0)),
            pl.BlockSpec((1, 1, d), lambda i, j: (j, 0, 0)),
            pl.BlockSpec((2, 1, d), lambda i, j: (0, 0, 0)),
            pl.BlockSpec((d, gw), lambda i, j: (0, j)),
            pl.BlockSpec((d, LANES), lambda i, j: (0, 0)),
        ],
        out_specs=[
            pl.BlockSpec((tr, gw), lambda i, j: (i, j)),
            pl.BlockSpec((tr, LANES), lambda i, j: (i, 0)),
        ],
        out_shape=[
            jax.ShapeDtypeStruct((rows, ncol), BF16),
            jax.ShapeDtypeStruct((rows, LANES), BF16),
        ],
        scratch_shapes=[pltpu.VMEM((tr, d), F32), pltpu.VMEM((tr, d), F32)],
        compiler_params=_cparams(("arbitrary", "arbitrary")),
        name="rwkv_inproj",
    )(x2, x2, x0, g, mu_main, mu_lora, w_main, w_lora)


def _bd(x, m0):
    zero = jnp.zeros_like(x)
    return jnp.concatenate([jnp.where(m0, x, zero), jnp.where(m0, zero, x)], axis=0)


def _wkv_kernel(r_ref, k_ref, v_ref, lo_ref, par_ref, w2a_ref, s0_ref, y_ref, sout_ref, s_sc, *, npairs):
    C = CHUNK
    di = npairs * LANES
    nch = r_ref.shape[1] // C

    @pl.when(pl.program_id(1) == 0)
    def _():
        s_sc[...] = s0_ref[...]

    row2 = lax.broadcasted_iota(jnp.int32, (LANES, LANES), 0)
    lane2 = lax.broadcasted_iota(jnp.int32, (LANES, LANES), 1)
    ones_bd = jnp.where((row2 // HEAD) == (lane2 // HEAD), 1.0, 0.0).astype(BF16)
    P = range(npairs)

    tri = jnp.where(lax.broadcasted_iota(jnp.int32, (C, C), 0)
                    >= lax.broadcasted_iota(jnp.int32, (C, C), 1), 1.0, 0.0).astype(BF16)
    tpos = lax.broadcasted_iota(jnp.int32, (C, LANES), 0)
    lane_c = lax.broadcasted_iota(jnp.int32, (C, LANES), 1)
    spos = lane_c % HEAD
    m0 = lane_c < HEAD
    strict = tpos > spos
    incl = tpos >= spos
    eye = jnp.where(tpos == spos, 1.0, 0.0).astype(F32)
    cat0 = lambda x, y: jnp.concatenate([x, y], axis=0)
    cat1 = lambda x, y: jnp.concatenate([x, y], axis=1)
    b16 = lambda x: x.astype(BF16)

    def head_sum_mxu(xs):
        st = _dot(jnp.concatenate(xs, axis=0), ones_bd)
        return [st[i * C:(i + 1) * C] for i in range(len(xs))]

    def head_sum(xs):
        out = []
        for x in xs:
            s0 = jnp.sum(jnp.where(m0, x, 0.0), axis=1, keepdims=True)
            s1 = jnp.sum(jnp.where(m0, 0.0, x), axis=1, keepdims=True)
            out.append(jnp.where(m0, s0, s1))
        return out

    lo = lo_ref[0].astype(F32)
    tl = b16(jnp.where(lax.broadcasted_iota(jnp.int32, lo.shape, 1) < HEAD, jnp.tanh(lo), lo))
    wa_all = [_dot(tl, cat1(w2a_ref[:, p * LANES:(p + 1) * LANES], w2a_ref[:, di + p * LANES:di + (p + 1) * LANES]))
              for p in P]

    def prep(c, pairs, out):
        rows = slice(c * C, (c + 1) * C)
        psl = {p: slice(p * LANES, (p + 1) * LANES) for p in pairs}
        par = lambda n, p: par_ref[n:n + 1, psl[p]]
        wa = {p: wa_all[p][rows] for p in pairs}
        k = {p: k_ref[0, rows, psl[p]].astype(F32) for p in pairs}
        kk = {p: k[p] * par(2, p) for p in pairs}
        n2 = dict(zip(pairs, head_sum([kk[p] * kk[p] for p in pairs])))
        yield
        lw = {p: -math.exp(-0.5) * jax.nn.sigmoid(par(0, p) + wa[p][:, :LANES]) for p in pairs}
        a = {p: jax.nn.sigmoid(par(1, p) + wa[p][:, LANES:]) for p in pairs}
        hi = {p: b16(lw[p]) for p in pairs}
        cs2 = {p: jnp.dot(tri, cat1(hi[p], b16(lw[p] - hi[p].astype(F32))), preferred_element_type=F32)
               for p in pairs}
        yield
        r = {p: r_ref[0, rows, psl[p]].astype(F32) for p in pairs}
        v = {p: v_ref[0, rows, psl[p]].astype(F32) for p in pairs}
        km = {p: k[p] * (1.0 + (a[p] - 1.0) * par(3, p)) for p in pairs}
        bsum = dict(zip(pairs, head_sum_mxu([r[p] * km[p] * par(4, p) for p in pairs])))
        yield
        for p in pairs:
            kkn = kk[p] * lax.rsqrt(jnp.maximum(n2[p], 1e-24))
            bv = kkn * a[p]
            cs = cs2[p][:, :LANES] + cs2[p][:, LANES:]
            last = cs[C - 1:C]
            e_out = jnp.exp(-cs)
            e_l = jnp.exp(last - cs)
            rt = r[p] * jnp.exp(cs)
            out[p] = dict(
                wc=jnp.exp(last), bonus=bsum[p] * v[p],
                vb=b16(v[p]), rtb=b16(rt), atb=b16(-kkn * jnp.exp(cs - lw[p])),
                kpb=b16(km[p] * e_l), bpb=b16(bv * e_l),
                zt=cat1(_bd(b16(bv * e_out), m0).T, _bd(b16(km[p] * e_out), m0).T))
        yield

    def staged(fn, out):
        for p in P:
            out.append(fn(p))
            if p % 4 == 3:
                yield

    def indep(d, res):
        sc = []
        yield from staged(lambda p: _dot(cat0(d[p]["atb"], d[p]["rtb"]), d[p]["zt"]), sc)
        ab = [jnp.where(strict, sc[p][:C, :LANES], 0.0) for p in P]
        akrk = [b16(cat0(jnp.where(strict, sc[p][:C, LANES:], 0.0), jnp.where(incl, sc[p][C:, LANES:], 0.0)))
                for p in P]
        res["rbb"] = [b16(jnp.where(incl, sc[p][C:, :LANES], 0.0)) for p in P]

        qb = [b16(ab[p]) for p in P]
        pm = [eye + ab[p] for p in P]
        q = []
        yield from staged(lambda p: _dot(qb[p], _bd(qb[p], m0)), q)
        for _ in range(4):
            qb = [b16(q[p]) for p in P]
            pq = []
            yield from staged(lambda p: _dot(cat0(b16(pm[p]), qb[p]), _bd(qb[p], m0)), pq)
            pm = [pm[p] + pq[p][:C] for p in P]
            q = [pq[p][C:] for p in P]
        res["pm"] = []
        yield from staged(lambda p: b16(pm[p] + _dot(pm[p], _bd(b16(q[p]), m0))), res["pm"])
        res["lv"] = []
        yield from staged(lambda p: _dot(akrk[p], _bd(d[p]["vb"], m0)), res["lv"])

    def dep(c, d, res, state):
        pm, lv, rbb = res["pm"], res["lv"], res["rbb"]
        ars = []
        def sbt(p):
            t = _bd(b16(state[p]), m0).T
            return cat0(t[:HEAD], t[HEAD:])
        yield from staged(lambda p: _dot(cat0(d[p]["atb"], d[p]["rtb"]), sbt(p)), ars)
        ub = []
        yield from staged(lambda p: b16(_dot(pm[p], _bd(b16(ars[p][:C] + lv[p][:C]), m0))), ub)
        ys = []
        yield from staged(lambda p: ars[p][C:] + lv[p][C:] + _dot(rbb[p], _bd(ub[p], m0)), ys)
        sfull = []
        yield from staged(lambda p: _dot(cat0(ub[p], d[p]["vb"]).T, cat0(d[p]["bpb"], d[p]["kpb"])), sfull)
        for p in P:
            state[p] = state[p] * d[p]["wc"] + jnp.where(m0, sfull[p][:HEAD], sfull[p][HEAD:])
        mean = head_sum(ys)
        dd = [ys[p] - mean[p] * (1.0 / HEAD) for p in P]
        yield
        var = [v_ * (1.0 / HEAD) for v_ in head_sum([dd[p] * dd[p] for p in P])]
        yield
        for p in P:
            ps = slice(p * LANES, (p + 1) * LANES)
            yn = dd[p] * lax.rsqrt(var[p] + GN_EPS) * par_ref[5:6, ps] + par_ref[6:7, ps] + d[p]["bonus"]
            y_ref[0, c * C:(c + 1) * C, ps] = yn.astype(y_ref.dtype)
            if p % 4 == 3:
                yield

    def prep_all(c, out):
        half = npairs // 2
        yield from prep(c, range(half), out)
        yield from prep(c, range(half, npairs), out)

    def interleave(gens):
        gens = list(gens)
        while gens:
            for g in list(gens):
                if next(g, StopIteration) is StopIteration:
                    gens.remove(g)

    state = [s_sc[p] for p in P]
    prepd = [[None] * npairs for _ in range(nch)]
    res = [dict() for _ in range(nch)]
    interleave([prep_all(0, prepd[0])])
    for ph in range(nch + 1):
        gens = []
        if ph >= 1:
            gens.append(dep(ph - 1, prepd[ph - 1], res[ph - 1], state))
        if ph < nch:
            gens.append(indep(prepd[ph], res[ph]))
        if ph + 1 < nch:
            gens.append(prep_all(ph + 1, prepd[ph + 1]))
        interleave(gens)
    for p in P:
        s_sc[p] = state[p]
        sout_ref[0, p] = state[p]


def _wkv(hp, hl, par, w2a, s0, *, batch, seq, tb):
    d_inner = hp.shape[1] // 4
    npairs = d_inner // LANES
    hp3 = hp.reshape(batch, seq, 4 * d_inner)
    hl3 = hl.reshape(batch, seq, LANES)
    kern = functools.partial(_wkv_kernel, npairs=npairs)
    return pl.pallas_call(
        kern,
        grid=(batch, seq // tb),
        in_specs=[
            pl.BlockSpec((1, tb, d_inner), lambda b, t: (b, t, 0)),
            pl.BlockSpec((1, tb, d_inner), lambda b, t: (b, t, 1)),
            pl.BlockSpec((1, tb, d_inner), lambda b, t: (b, t, 2)),
            pl.BlockSpec((1, tb, LANES), lambda b, t: (b, t, 0)),
            pl.BlockSpec((8, d_inner), lambda b, t: (0, 0)),
            pl.BlockSpec((LANES, 2 * d_inner), lambda b, t: (0, 0)),
            pl.BlockSpec((npairs, HEAD, LANES), lambda b, t: (0, 0, 0)),
        ],
        out_specs=[
            pl.BlockSpec((1, tb, d_inner), lambda b, t: (b, t, 0)),
            pl.BlockSpec((1, npairs, HEAD, LANES), lambda b, t: (b, 0, 0, 0)),
        ],
        out_shape=[
            jax.ShapeDtypeStruct((batch, seq, d_inner), BF16),
            jax.ShapeDtypeStruct((batch, npairs, HEAD, LANES), F32),
        ],
        scratch_shapes=[pltpu.VMEM((npairs, HEAD, LANES), F32)],
        compiler_params=_cparams(("arbitrary", "arbitrary")),
        name="wkv7",
    )(hp3, hp3, hp3, hl3, par, w2a, s0)


def _outproj_kernel(y_ref, g_ref, res_ref, w_ref, gn_ref, out_ref):
    g = g_ref[...].astype(F32)
    z = y_ref[...].astype(F32) * (g * jax.nn.sigmoid(g))
    m = _dot(z, w_ref[...])
    out_ref[...] = res_ref[...] + _rms(m, gn_ref[...])


def _outproj(y2, gsrc, gcol, res2, w, gn, *, tr):
    rows, di = y2.shape
    d = w.shape[1]
    return pl.pallas_call(
        _outproj_kernel,
        grid=(rows // tr,),
        in_specs=[
            pl.BlockSpec((tr, di), lambda i: (i, 0)),
            pl.BlockSpec((tr, di), lambda i: (i, gcol)),
            pl.BlockSpec((tr, d), lambda i: (i, 0)),
            pl.BlockSpec((di, d), lambda i: (0, 0)),
            pl.BlockSpec((1, d), lambda i: (0, 0)),
        ],
        out_specs=pl.BlockSpec((tr, d), lambda i: (i, 0)),
        out_shape=jax.ShapeDtypeStruct((rows, d), F32),
        compiler_params=_cparams(("arbitrary",)),
        name="gate_outproj",
    )(y2, gsrc, res2, w, gn)


def _mlaproj_kernel(h_ref, gpre_ref, win_ref, qn_ref, wq_ref, kvn_ref, wkn_ref, wvt_ref, rc_ref, rs_ref,
                    oqn_ref, oqr_ref, okn_ref, ovt_ref, okr_ref, og_ref, *, scale):
    u = _rms(h_ref[...], gpre_ref[...])
    t = _dot(u, win_ref[...])
    nq = qn_ref.shape[1]
    nkv = kvn_ref.shape[1]
    di = og_ref.shape[1]
    c_q = t[:, :nq]
    c_kv = t[:, nq:nq + nkv]
    krr = t[:, nq + nkv:nq + nkv + LANES]
    og_ref[...] = t[:, nq + nkv + LANES:].astype(og_ref.dtype)

    rc = rc_ref[...]
    rs = rs_ref[...]
    lane = lax.broadcasted_iota(jnp.int32, krr.shape, 1)
    prod = krr * jnp.where(lane < QK_ROPE, rc, rs)
    okr_ref[...] = (prod + pltpu.roll(prod, QK_ROPE, 1)).astype(okr_ref.dtype)

    q = _dot(_rms(c_q, qn_ref[...]), wq_ref[...]) * scale
    oqn_ref[...] = q[:, :di].astype(oqn_ref.dtype)
    nrep = (di // 2) // LANES
    rct = jnp.concatenate([rc] * nrep, axis=1)
    rst = jnp.concatenate([rs] * nrep, axis=1)
    oqr_ref[...] = (q[:, di:di + di // 2] * rct + q[:, di + di // 2:] * rst).astype(oqr_ref.dtype)

    ckv = _rms(c_kv, kvn_ref[...])
    okn_ref[...] = _dot(ckv, wkn_ref[...]).astype(okn_ref.dtype)
    ovt_ref[...] = _dot_nt(wvt_ref[...], ckv).astype(ovt_ref.dtype)


def _mlaproj(h2, gpre, win, qn, wq, kvn, wkn, wvt, rc, rs, *, tr, scale):
    rows, d = h2.shape
    di = wkn.shape[1]
    full = lambda a: pl.BlockSpec(a.shape, lambda i: (0, 0), pipeline_mode=pl.Buffered(1))
    rowblk = lambda n: pl.BlockSpec((tr, n), lambda i: (i, 0))
    tps = rc.shape[0] // tr
    rope = pl.BlockSpec((tr, LANES), lambda i: (i % tps, 0))
    kern = functools.partial(_mlaproj_kernel, scale=scale)
    return pl.pallas_call(
        kern,
        grid=(rows // tr,),
        in_specs=[rowblk(d), full(gpre), full(win), full(qn), full(wq), full(kvn), full(wkn), full(wvt),
                  rope, rope],
        out_specs=[rowblk(di), rowblk(di // 2), rowblk(di), pl.BlockSpec((di, tr), lambda i: (0, i)),
                   rowblk(LANES), rowblk(di)],
        out_shape=[
            jax.ShapeDtypeStruct((rows, di), BF16),
            jax.ShapeDtypeStruct((rows, di // 2), BF16),
            jax.ShapeDtypeStruct((rows, di), BF16),
            jax.ShapeDtypeStruct((di, rows), BF16),
            jax.ShapeDtypeStruct((rows, LANES), BF16),
            jax.ShapeDtypeStruct((rows, di), BF16),
        ],
        compiler_params=_cparams(("arbitrary",)),
        name="mla_proj",
    )(h2, gpre, win, qn, wq, kvn, wkn, wvt, rc, rs)


NEG = -1e30


def _attn_kernel(qn_ref, qr_ref, kn_ref, kr_ref, vt_ref, knm_ref, krm_ref, vtm_ref, o_ref, s_sc, qt_sc, *,
                 tq, nmeta):
    qi = pl.program_id(2)
    mp = knm_ref.shape[1]
    nh = qn_ref.shape[2] // LANES
    H = range(nh)
    lane = lax.broadcasted_iota(jnp.int32, (tq, LANES), 1)
    hsl = [slice(h * LANES, (h + 1) * LANES) for h in H]

    def query(h):
        qr = qr_ref[0, :, hsl[h // 2]]
        return jnp.concatenate([qn_ref[0, :, hsl[h]],
                                jnp.where((lane // QK_ROPE) == h % 2, qr, jnp.zeros_like(qr))], axis=1)

    for h in H:
        qt_sc[h] = query(h).T

    def scores(j, h):
        rows = pl.ds(pl.multiple_of(j * tq, tq), tq)
        return _dot(jnp.concatenate([kn_ref[0, rows, hsl[h]], kr_ref[0, rows, :]], axis=1), qt_sc[h])

    def update(carry, s, smax, vt):
        m, acc = carry
        mn = jnp.maximum(m, smax)
        p = jnp.exp2(s - mn)
        vta = jnp.concatenate([vt, jnp.ones((16, vt.shape[1]), BF16)], axis=0)
        return [mn, jnp.exp2(m - mn) * acc + _dot(vta, p)]

    colmax = lambda s: jnp.max(s, axis=0, keepdims=True)

    bmax = []
    for h in H:
        s0 = scores(0, h)
        s_sc[h] = s0
        bmax.append(colmax(s0))

    def step(j, carry):
        rows = pl.ds(pl.multiple_of(j * tq, tq), tq)
        out = []
        for h in H:
            nxt = scores(j + 1, h)
            out += update(carry[3 * h:3 * h + 2], s_sc[h], carry[3 * h + 2], vt_ref[0, hsl[h], rows])
            s_sc[h] = nxt
            out.append(colmax(nxt))
        return out

    row1 = lambda val: jnp.full((1, tq), val, F32)
    carry = []
    for h in H:
        carry += [row1(NEG), jnp.zeros((LANES + 16, tq), F32), bmax[h]]
    carry = lax.fori_loop(0, qi, step, carry)

    krow = lax.broadcasted_iota(jnp.int32, (tq, tq), 0)
    qcol = lax.broadcasted_iota(jnp.int32, (tq, tq), 1)
    causal = krow <= qcol
    drows = pl.ds(pl.multiple_of(qi * tq, tq), tq)
    sm = [_dot(jnp.concatenate([knm_ref[0, :, hsl[h]], krm_ref[0]], axis=1), qt_sc[h])[:nmeta] for h in H]
    pad = jnp.zeros((mp - nmeta, tq), F32)
    for h in H:
        sd = jnp.where(causal, s_sc[h], NEG)
        m, acc = carry[3 * h:3 * h + 2]
        mn = jnp.maximum(m, jnp.maximum(colmax(sm[h]), colmax(sd)))
        p = jnp.concatenate([jnp.exp2(sm[h] - mn), pad, jnp.exp2(sd - mn)], axis=0)
        vta = jnp.concatenate([jnp.concatenate([vtm_ref[0, hsl[h], :], vt_ref[0, hsl[h], drows]], axis=1),
                               jnp.ones((16, mp + tq), BF16)], axis=0)
        acc = jnp.exp2(m - mn) * acc + _dot(vta, p)
        o_ref[0, :, hsl[h]] = (acc[:LANES] / acc[LANES:LANES + 1]).T.astype(o_ref.dtype)


def _attention(qn, qr, kn, kr, vt, knm, krm, vtm, *, batch, seq, tq, nmeta, nh):
    di = qn.shape[1]
    hw = nh * LANES
    r3 = lambda a: a.reshape(batch, seq, a.shape[1])
    mp = knm.shape[0]
    kern = functools.partial(_attn_kernel, tq=tq, nmeta=nmeta)
    return pl.pallas_call(
        kern,
        grid=(batch, di // hw, seq // tq),
        in_specs=[
            pl.BlockSpec((1, tq, hw), lambda b, p, i: (b, i, p)),
            pl.BlockSpec((1, tq, hw // 2), lambda b, p, i: (b, i, p)),
            pl.BlockSpec((1, seq, hw), lambda b, p, i: (b, 0, p)),
            pl.BlockSpec((1, seq, LANES), lambda b, p, i: (b, 0, 0)),
            pl.BlockSpec((1, hw, seq), lambda b, p, i: (0, p, b)),
            pl.BlockSpec((1, mp, hw), lambda b, p, i: (0, 0, p)),
            pl.BlockSpec((1, mp, LANES), lambda b, p, i: (0, 0, 0)),
            pl.BlockSpec((1, hw, mp), lambda b, p, i: (0, p, 0)),
        ],
        out_specs=pl.BlockSpec((1, tq, hw), lambda b, p, i: (b, i, p)),
        out_shape=jax.ShapeDtypeStruct((batch, seq, di), BF16),
        scratch_shapes=[pltpu.VMEM((nh, tq, tq), F32), pltpu.VMEM((nh, 2 * LANES, tq), BF16)],
        compiler_params=_cparams(("arbitrary", "arbitrary", "arbitrary")),
        name="mla_attention",
    )(r3(qn), r3(qr), r3(kn), r3(kr), vt[None], knm[None], krm[None], vtm[None])


def _pick(n, cands):
    for c in cands:
        if n % c == 0:
            return c
    return n


def kernel(x, meta_tokens, norm_pre, norm_post, rwkv_mu, rwkv_w_in, rwkv_w0, rwkv_w2, rwkv_a0, rwkv_a2,
           rwkv_k_k, rwkv_k_a, rwkv_r_k, rwkv_ln_w, rwkv_ln_b, rwkv_w_out,
           mla_w_in, mla_q_norm, mla_w_q_up, mla_kv_norm, mla_w_kv_up, mla_w_out):
    B, T, D = x.shape
    DI = rwkv_w_out.shape[1]
    NP = DI // LANES
    MH = DI // V_HEAD
    nq = mla_q_norm.shape[1]
    nkv = mla_kv_norm.shape[1]
    MP = CHUNK
    assert meta_tokens.shape[0] == N_META and T % CHUNK == 0

    w_in = rwkv_w_in[0]
    w_main = w_in[:, :4 * DI].astype(BF16)
    w_lora = w_in[:, 4 * DI:].astype(BF16)
    mu_main = rwkv_mu[0, :4].reshape(4, 1, D)
    mu_lora = rwkv_mu[0, 4:].reshape(2, 1, D)
    zeros1 = jnp.zeros((DI,), F32)
    par = jnp.stack([rwkv_w0[0], rwkv_a0[0], rwkv_k_k[0], rwkv_k_a[0], rwkv_r_k[0],
                     rwkv_ln_w[0], rwkv_ln_b[0], zeros1])
    zblk = jnp.zeros_like(rwkv_w2[0])
    w2a = jnp.concatenate([jnp.concatenate([rwkv_w2[0], zblk], axis=1),
                           jnp.concatenate([zblk, rwkv_a2[0]], axis=1)], axis=0).astype(BF16)
    w_out0 = rwkv_w_out[0].astype(BF16)

    mw = mla_w_in[0]
    kr_w = mw[:, nq + nkv:nq + nkv + QK_ROPE]
    half = QK_ROPE // 2
    rot = lambda wr: jnp.concatenate([-wr[..., half:], wr[..., :half]], axis=-1)
    win2 = jnp.concatenate([mw[:, :nq + nkv], kr_w, rot(kr_w), mw[:, nq + nkv + QK_ROPE:]],
                           axis=1).astype(BF16)
    wq = mla_w_q_up[0].reshape(nq, MH, QK_NOPE + QK_ROPE)
    wq_r = wq[:, :, QK_NOPE:]
    wq2 = jnp.concatenate([wq[:, :, :QK_NOPE].reshape(nq, MH * QK_NOPE),
                           wq_r.reshape(nq, MH * QK_ROPE),
                           rot(wq_r).reshape(nq, MH * QK_ROPE)], axis=1).astype(BF16)
    wkv = mla_w_kv_up[0].reshape(nkv, MH, QK_NOPE + V_HEAD)
    wkn = wkv[:, :, :QK_NOPE].reshape(nkv, MH * QK_NOPE).astype(BF16)
    wvt = wkv[:, :, QK_NOPE:].reshape(nkv, MH * V_HEAD).T.astype(BF16)
    w_out1 = mla_w_out[0].astype(BF16)

    pos = jnp.arange(N_META + T, dtype=F32)
    inv_freq = jnp.exp(-math.log(ROPE_THETA) * jnp.arange(half, dtype=F32) / half)
    ang = pos[:, None] * inv_freq[None, :]
    rc = jnp.tile(jnp.cos(ang), (1, 4))
    rs = jnp.tile(jnp.sin(ang), (1, 4))
    scale = (QK_NOPE + QK_ROPE) ** -0.5 * math.log2(math.e)

    gpre0 = norm_pre[0:1]
    gpre1 = norm_pre[1:2]
    gpost0 = norm_post[0:1]
    gpost1 = norm_post[1:2]

    xm = jnp.concatenate([jnp.zeros((MP - N_META, D), x.dtype), meta_tokens.astype(x.dtype)], axis=0)
    zero8 = jnp.zeros((8, D), x.dtype)
    hp_m, hl_m = _inproj(xm, zero8, gpre0, mu_main, mu_lora, w_main, w_lora, seq=MP, tr=MP)
    s_zero = jnp.zeros((NP, HEAD, LANES), F32)
    y_m, s_meta = _wkv(hp_m, hl_m, par, w2a, s_zero, batch=1, seq=MP, tb=MP)
    h1_m = _outproj(y_m.reshape(MP, DI), hp_m, 3, xm, w_out0, gpost0, tr=MP)
    rc_m = jnp.concatenate([jnp.zeros((MP - N_META, LANES), F32), rc[:N_META]], axis=0)
    rs_m = jnp.concatenate([jnp.zeros((MP - N_META, LANES), F32), rs[:N_META]], axis=0)
    _, _, kn_m, vt_m, kr_m, _ = _mlaproj(h1_m, gpre1, win2, mla_q_norm, wq2, mla_kv_norm, wkn, wvt,
                                         rc_m, rs_m, tr=MP, scale=scale)
    padrows = lambda a: jnp.concatenate(
        [a[MP - N_META:], jnp.zeros((LANES - N_META, a.shape[1]), a.dtype)], axis=0)
    knm, krm = padrows(kn_m), padrows(kr_m)
    vtm = jnp.concatenate([vt_m[:, MP - N_META:], jnp.zeros((DI, LANES - N_META), vt_m.dtype)], axis=1)

    x2 = x.reshape(B * T, D)
    tr_in = _pick(T, (1024, 512, 256, 128, 64))
    hp, hl = _inproj(x2, meta_tokens[N_META - 8:].astype(x.dtype), gpre0, mu_main, mu_lora,
                     w_main, w_lora, seq=T, tr=tr_in)
    y, _ = _wkv(hp, hl, par, w2a, s_meta[0], batch=B, seq=T, tb=_pick(T, (4 * CHUNK, 2 * CHUNK, CHUNK)))
    tr_o = _pick(T, (1024, 512, 256, 128, 64))
    h1 = _outproj(y.reshape(B * T, DI), hp, 3, x2, w_out0, gpost0, tr=tr_o)

    tr_p = _pick(T, (512, 256, 128, 64))
    qn, qr, kn, vt, kr, g1 = _mlaproj(h1, gpre1, win2, mla_q_norm, wq2, mla_kv_norm, wkn, wvt,
                                      rc[N_META:], rs[N_META:], tr=tr_p, scale=scale)
    tq = _pick(T, (512, 256, 128))
    o = _attention(qn, qr, kn, kr, vt, knm, krm, vtm, batch=B, seq=T, tq=tq, nmeta=N_META, nh=4)
    out = _outproj(o.reshape(B * T, DI), g1, 0, h1, w_out1, gpost1, tr=tr_o)
    return out.reshape(B, T, D)
```
